```python
import math
import jax, jax.numpy as jnp
from jax import lax
import numpy as np

D_MODEL = 1024
BATCH = 32
SEQ = 2048
DEPTH = 4

D_MIX = D_MODEL
CONV_GROUPS = 4
CONV_WIDTH = D_MIX // 4
CONV_K = 3
SB_HEADS = 6
SB_HEAD_DIM = 64
SB_WIDTH = SB_HEADS * SB_HEAD_DIM
SB_BLOCK = 128
GLA_HEADS = 4
GLA_WIDTH = D_MIX - CONV_WIDTH - SB_WIDTH
GLA_DV = GLA_WIDTH // GLA_HEADS
GLA_DK = GLA_DV // 2
GLA_KEY_WIDTH = GLA_HEADS * GLA_DK
GLA_GATE_RANK = 16
GLA_GATE_NORM = 16.0
GLA_CHUNK = 64
PROJ_SIZES = (CONV_WIDTH, CONV_WIDTH, CONV_WIDTH,
              SB_WIDTH, SB_WIDTH, SB_WIDTH,
              GLA_KEY_WIDTH, GLA_KEY_WIDTH, GLA_WIDTH, GLA_WIDTH, GLA_GATE_RANK)
D_PROJ = 3 * CONV_WIDTH + 3 * SB_WIDTH + 2 * GLA_KEY_WIDTH + 2 * GLA_WIDTH + GLA_GATE_RANK
N_EXPERTS = 16
N_GROUPS = 4
EXPERTS_PER_GROUP = N_EXPERTS // N_GROUPS
TOP_K = 2
D_EXPERT = 512
ALPHA = (2 * DEPTH) ** 0.25
BETA = (8 * DEPTH) ** -0.25
LN_EPS = 1e-5
RMS_EPS = 1e-6

kernel_name = "hybrid_conv_sb_gla_grouped_moe_deepnorm"


def layer_norm(x, g, b):
    xf = x.astype(jnp.float32)
    mu = jnp.mean(xf, axis=-1, keepdims=True)
    var = jnp.mean(jnp.square(xf - mu), axis=-1, keepdims=True)
    return ((xf - mu) * lax.rsqrt(var + LN_EPS) * g + b).astype(x.dtype)


def split_projection(proj):
    idx = []
    acc = 0
    for s in PROJ_SIZES[:-1]:
        acc += s
        idx.append(acc)
    return jnp.split(proj, idx, axis=-1)


def short_conv_mixer(h, b_gate, c_gate, conv_w, conv_b):
    u = c_gate * h
    S = u.shape[1]
    u_pad = jnp.pad(u, ((0, 0), (CONV_K - 1, 0), (0, 0)))
    y = conv_b
    for i in range(CONV_K):
        y = y + conv_w[i] * u_pad[:, CONV_K - 1 - i:CONV_K - 1 - i + S]
    return b_gate * y


def stick_breaking_attention(q, k, v):
    Bsz, S, H, Dh = q.shape
    q = q.transpose(0, 2, 1, 3)
    k = k.transpose(0, 2, 1, 3)
    v = v.transpose(0, 2, 1, 3)
    scale = Dh ** -0.5
    outs = []
    for blk in range(S // SB_BLOCK):
        q0 = blk * SB_BLOCK
        q1 = q0 + SB_BLOCK
        qb = q[:, :, q0:q1]
        kb = k[:, :, :q1]
        vb = v[:, :, :q1]
        z = jnp.einsum('bhqd,bhkd->bhqk', qb, kb).astype(jnp.float32) * scale
        t_idx = jnp.arange(q0, q1)[:, None]
        s_idx = jnp.arange(q1)[None, :]
        causal = s_idx < t_idx
        log_beta = jax.nn.log_sigmoid(z)
        log_1m_beta = jnp.where(causal, jax.nn.log_sigmoid(-z), 0.0)
        incl = jnp.cumsum(log_1m_beta, axis=-1)
        later = incl[..., -1:] - incl
        a = jnp.where(causal, jnp.exp(log_beta + later), 0.0)
        outs.append(jnp.einsum('bhqk,bhkd->bhqd', a.astype(v.dtype), vb))
    o = jnp.concatenate(outs, axis=2)
    return o.transpose(0, 2, 1, 3)


def gla_mixer(q, k, v, g, gate_lr, w_gate2, b_gate, norm_w):
    Bsz, S, H, DK = q.shape
    DV = v.shape[-1]
    C = GLA_CHUNK
    nC = S // C
    log_a = jax.nn.log_sigmoid((gate_lr @ w_gate2 + b_gate).astype(jnp.float32)) / GLA_GATE_NORM
    log_a = log_a.reshape(Bsz, S, H, DK)

    def chunk(t):
        return t.reshape(Bsz, nC, C, H, -1).transpose(0, 3, 1, 2, 4)

    qc = chunk(q).astype(jnp.float32) * (DK ** -0.5)
    kc = chunk(k).astype(jnp.float32)
    vc = chunk(v).astype(jnp.float32)
    bcum = jnp.cumsum(chunk(log_a), axis=3)
    b_last = bcum[:, :, :, -1:]
    q_in = qc * jnp.exp(bcum)
    k_in = kc * jnp.exp(-bcum)
    k_to_end = kc * jnp.exp(b_last - bcum)
    scores = jnp.einsum('bhcid,bhcjd->bhcij', q_in, k_in)
    tri = jnp.tril(jnp.ones((C, C), dtype=bool))
    scores = jnp.where(tri, scores, 0.0)
    o_intra = jnp.einsum('bhcij,bhcje->bhcie', scores, vc)
    kv = jnp.einsum('bhcjd,bhcje->bhcde', k_to_end, vc)
    decay = jnp.exp(b_last[:, :, :, 0])

    def step(state, inp):
        kv_c, dec_c = inp
        return dec_c[..., None] * state + kv_c, state

    init = jnp.zeros((Bsz, H, DK, DV), jnp.float32)
    _, s_prev = lax.scan(step, init, (jnp.moveaxis(kv, 2, 0), jnp.moveaxis(decay, 2, 0)))
    s_prev = jnp.moveaxis(s_prev, 0, 2)
    o_inter = jnp.einsum('bhcid,bhcde->bhcie', q_in, s_prev)
    o = (o_intra + o_inter).transpose(0, 2, 3, 1, 4).reshape(Bsz, S, H, DV)
    o_n = o * lax.rsqrt(jnp.mean(jnp.square(o), axis=-1, keepdims=True) + RMS_EPS) * norm_w
    out = o_n * jax.nn.silu(g.astype(jnp.float32))
    return out.reshape(Bsz, S, H * DV).astype(v.dtype)


def grouped_moe(x, router_w, router_bias, w1, w3, w2):
    Bsz, S, D = x.shape
    t = x.reshape(-1, D)
    T = t.shape[0]
    scores = jax.nn.sigmoid((t @ router_w).astype(jnp.float32))
    sel = scores + router_bias.astype(jnp.float32)
    grp = sel.reshape(T, N_GROUPS, EXPERTS_PER_GROUP)
    grp_score = jnp.sum(lax.top_k(grp, 2)[0], axis=-1)
    _, top_group = lax.top_k(grp_score, 1)
    group_mask = jax.nn.one_hot(top_group[:, 0], N_GROUPS, dtype=jnp.bool_)
    expert_mask = jnp.repeat(group_mask, EXPERTS_PER_GROUP, axis=1)
    masked_sel = jnp.where(expert_mask, sel, -jnp.inf)
    _, top_idx = lax.top_k(masked_sel, TOP_K)
    top_w = jnp.take_along_axis(scores, top_idx, axis=1)
    top_w = top_w / jnp.sum(top_w, axis=-1, keepdims=True)
    gate = jnp.sum(jax.nn.one_hot(top_idx, N_EXPERTS, dtype=jnp.float32) * top_w[..., None], axis=1)
    gate = gate.astype(t.dtype)
    y = jnp.zeros_like(t)
    for e in range(N_EXPERTS):
        h = jax.nn.silu(t @ w1[e]) * (t @ w3[e])
        y = y + gate[:, e:e + 1] * (h @ w2[e])
    return y.reshape(Bsz, S, D)


def setup_inputs(seed: int = 0) -> dict:
    key = jax.random.key(seed)
    ks = jax.random.split(key, 20)
    f32 = jnp.float32
    nrm = lambda k, shape, s: jax.random.normal(k, shape, f32) * s
    x = nrm(ks[0], (BATCH, SEQ, D_MODEL), 1.0)
    w_in = nrm(ks[1], (DEPTH, D_MODEL, D_PROJ), D_MODEL ** -0.5)
    conv_w = nrm(ks[2], (DEPTH, CONV_K, CONV_WIDTH), CONV_K ** -0.5)
    conv_b = nrm(ks[3], (DEPTH, CONV_WIDTH), 0.01)
    gla_w_gate2 = nrm(ks[4], (DEPTH, GLA_GATE_RANK, GLA_KEY_WIDTH), GLA_GATE_RANK ** -0.5)
    gla_b_gate = nrm(ks[5], (DEPTH, GLA_KEY_WIDTH), 0.01)
    gla_norm_w = 1.0 + nrm(ks[6], (DEPTH, GLA_DV), 0.01)
    w_out = nrm(ks[7], (DEPTH, D_MIX, D_MODEL), D_MIX ** -0.5 * BETA)
    ln1_g = 1.0 + nrm(ks[8], (DEPTH, D_MODEL), 0.01)
    ln1_b = nrm(ks[9], (DEPTH, D_MODEL), 0.01)
    router_w = nrm(ks[10], (D_MODEL, N_EXPERTS), D_MODEL ** -0.5)
    router_bias = nrm(ks[11], (N_EXPERTS,), 0.01)
    w1 = nrm(ks[12], (DEPTH, N_EXPERTS, D_MODEL, D_EXPERT), D_MODEL ** -0.5)
    w3 = nrm(ks[13], (DEPTH, N_EXPERTS, D_MODEL, D_EXPERT), D_MODEL ** -0.5)
    w2 = nrm(ks[14], (DEPTH, N_EXPERTS, D_EXPERT, D_MODEL), D_EXPERT ** -0.5 * BETA)
    ln2_g = 1.0 + nrm(ks[15], (DEPTH, D_MODEL), 0.01)
    ln2_b = nrm(ks[16], (DEPTH, D_MODEL), 0.01)
    return {"x": x, "w_in": w_in, "conv_w": conv_w, "conv_b": conv_b,
            "gla_w_gate2": gla_w_gate2, "gla_b_gate": gla_b_gate, "gla_norm_w": gla_norm_w,
            "w_out": w_out, "ln1_g": ln1_g, "ln1_b": ln1_b,
            "router_w": router_w, "router_bias": router_bias,
            "w1": w1, "w3": w3, "w2": w2, "ln2_g": ln2_g, "ln2_b": ln2_b}


def reference(x, w_in, conv_w, conv_b, gla_w_gate2, gla_b_gate, gla_norm_w, w_out,
              ln1_g, ln1_b, router_w, router_bias, w1, w3, w2, ln2_g, ln2_b):
    Bsz, S, D = x.shape
    for l in range(DEPTH):
        proj = x @ w_in[l]
        (c_h, c_b, c_c, sb_q, sb_k, sb_v,
         g_q, g_k, g_v, g_g, g_lr) = split_projection(proj)
        y_conv = short_conv_mixer(c_h, c_b, c_c, conv_w[l], conv_b[l])
        y_sb = stick_breaking_attention(
            sb_q.reshape(Bsz, S, SB_HEADS, SB_HEAD_DIM),
            sb_k.reshape(Bsz, S, SB_HEADS, SB_HEAD_DIM),
            sb_v.reshape(Bsz, S, SB_HEADS, SB_HEAD_DIM)).reshape(Bsz, S, SB_WIDTH)
        y_gla = gla_mixer(
            g_q.reshape(Bsz, S, GLA_HEADS, GLA_DK),
            g_k.reshape(Bsz, S, GLA_HEADS, GLA_DK),
            g_v.reshape(Bsz, S, GLA_HEADS, GLA_DV),
            g_g.reshape(Bsz, S, GLA_HEADS, GLA_DV),
            g_lr, gla_w_gate2[l], gla_b_gate[l], gla_norm_w[l])
        mix = jnp.concatenate([y_conv, y_sb, y_gla], axis=-1) @ w_out[l]
        x = layer_norm(ALPHA * x + mix, ln1_g[l], ln1_b[l])
        ffn = grouped_moe(x, router_w, router_bias, w1[l], w3[l], w2[l])
        x = layer_norm(ALPHA * x + ffn, ln2_g[l], ln2_b[l])
    return x
```

```python
import functools
import math

import jax
import jax.numpy as jnp
from jax import lax
from jax.experimental import pallas as pl
from jax.experimental.pallas import tpu as pltpu

f32 = jnp.float32
bf16 = jnp.bfloat16
i32 = jnp.int32

V7X_LANES = 128
V7X_SUBLANES = 8
V7X_VMEM_BYTES = 64 * 1024 * 1024

D_MODEL = 1024
CONV_WIDTH = 256
CONV_K = 3
SB_HEADS = 6
SB_HEAD_DIM = 64
SB_WIDTH = SB_HEADS * SB_HEAD_DIM
SB_BLOCK = 128
GLA_HEADS = 4
GLA_DV = 96
GLA_DK = 48
GLA_WIDTH = GLA_HEADS * GLA_DV
GLA_KEY_WIDTH = GLA_HEADS * GLA_DK
GLA_KEY_PAD = 256
GLA_GATE_RANK = 16
GLA_GATE_NORM = 16.0
GLA_CHUNK = 64
N_EXPERTS = 16
N_GROUPS = 4
EXPERTS_PER_GROUP = 4
D_EXPERT = 512
LN_EPS = 1e-5
RMS_EPS = 1e-6
LOG2E = 1.4426950408889634

OFF_CONV = 0
OFF_SBQ = 3 * CONV_WIDTH
OFF_SBK = OFF_SBQ + SB_WIDTH
OFF_SBV = OFF_SBK + SB_WIDTH
OFF_GQ = OFF_SBV + SB_WIDTH
OFF_GK = OFF_GQ + GLA_KEY_PAD
OFF_GV = OFF_GK + GLA_KEY_PAD
OFF_GG = OFF_GV + GLA_WIDTH
OFF_LR = OFF_GG + GLA_WIDTH
P_WIDTH = OFF_LR + V7X_LANES

ROW_TILE = V7X_SUBLANES


def _vmem_params(est_bytes, semantics):
    limit = int(min(V7X_VMEM_BYTES - 6 * 1024 * 1024, max(est_bytes, 16 * 1024 * 1024)))
    return pltpu.CompilerParams(dimension_semantics=semantics, vmem_limit_bytes=limit)


def _neg_abs(x):
    bits = lax.bitcast_convert_type(x, jnp.uint32) | jnp.uint32(0x80000000)
    return lax.bitcast_convert_type(bits, f32)


def _rows_to_tiles_store(ref, val, n_rows):
    for j in range(D_MODEL // V7X_LANES):
        ref[pl.ds(j, n_rows, stride=ROW_TILE), :] = val[:, j * V7X_LANES:(j + 1) * V7X_LANES]


def _tiles_to_rows_load(ref, n_rows):
    return jnp.concatenate(
        [ref[pl.ds(j, n_rows, stride=ROW_TILE), :] for j in range(D_MODEL // V7X_LANES)], axis=1)


def _layer_norm(h, g, b):
    mu = jnp.mean(h, axis=-1, keepdims=True)
    c = h - mu
    var = jnp.mean(c * c, axis=-1, keepdims=True)
    return c * lax.rsqrt(var + LN_EPS) * g + b


def _in_proj_kernel(x_ref, w_ref, o_ref):
    xb = x_ref[...].astype(bf16)
    n_chunk = 256
    for n0 in range(0, P_WIDTH, n_chunk):
        o_ref[:, n0:n0 + n_chunk] = jnp.dot(
            xb, w_ref[:, n0:n0 + n_chunk], preferred_element_type=f32).astype(bf16)


def _in_proj(x2d, w):
    t = x2d.shape[0]
    tm = 512
    est = 2 * (tm * D_MODEL * 4 + D_MODEL * P_WIDTH * 2 + tm * P_WIDTH * 2) + 8 * 1024 * 1024
    return pl.pallas_call(
        _in_proj_kernel,
        grid=(t // tm,),
        in_specs=[pl.BlockSpec((tm, D_MODEL), lambda i: (i, 0)),
                  pl.BlockSpec((D_MODEL, P_WIDTH), lambda i: (0, 0))],
        out_specs=pl.BlockSpec((tm, P_WIDTH), lambda i: (i, 0)),
        out_shape=jax.ShapeDtypeStruct((t, P_WIDTH), bf16),
        compiler_params=_vmem_params(est, ("arbitrary",)),
        name="in_proj",
    )(x2d, w)


def _sb_kernel(q_ref, k_ref, v_ref, o_ref, acc_ref, car_ref, *, seq):
    blk = SB_BLOCK
    nqb = seq // blk
    row = lax.broadcasted_iota(i32, (blk, blk), 0)
    col = lax.broadcasted_iota(i32, (blk, blk), 1)
    causal = col < row
    head_mask = [col < SB_HEAD_DIM, col >= SB_HEAD_DIM]
    wr = lax.broadcasted_iota(i32, (2 * blk, 2 * blk), 0)
    wc = lax.broadcasted_iota(i32, (2 * blk, 2 * blk), 1)
    wr = jnp.where(wr >= blk, wr - blk, wr)
    tri_w = jnp.where((wc >= blk) | (wr >= wc), 1.0, 0.0).astype(bf16)

    def softplus2(z):
        return jnp.maximum(z, 0.0) + jnp.log2(1.0 + jnp.exp2(_neg_abs(z)))

    def suffix_sums(p):
        hi = p.astype(bf16)
        lo = (p - hi.astype(f32)).astype(bf16)
        r = jnp.dot(jnp.concatenate([hi, lo], axis=1), tri_w, preferred_element_type=f32)
        return r[:, :blk], r[:, blk:]

    def q_block(qb, carry):
        q0 = pl.multiple_of(qb * blk, blk)
        q = q_ref[pl.ds(q0, blk), :]
        qh = [jnp.where(head_mask[h], q, jnp.zeros_like(q)) for h in range(2)]
        k = k_ref[pl.ds(q0, blk), :]
        v = v_ref[pl.ds(q0, blk), :]
        for h in range(2):
            z = lax.dot_general(qh[h], k, (((1,), (1,)), ((), ())), preferred_element_type=f32)
            p = jnp.where(causal, softplus2(z), 0.0)
            s_in, r_tot = suffix_sums(p)
            a = jnp.where(causal, jnp.exp2(z - s_in), 0.0)
            acc_ref[h] = jnp.dot(a.astype(bf16), v, preferred_element_type=f32)
            car_ref[h] = r_tot

        def k_block(i, c):
            k0 = pl.multiple_of((qb - 1 - i) * blk, blk)
            kk = k_ref[pl.ds(k0, blk), :]
            vv = v_ref[pl.ds(k0, blk), :]
            for h in range(2):
                z = lax.dot_general(qh[h], kk, (((1,), (1,)), ((), ())), preferred_element_type=f32)
                s_in, r_tot = suffix_sums(softplus2(z))
                car = car_ref[h]
                a = jnp.exp2(z - (s_in + car))
                acc_ref[h] += jnp.dot(a.astype(bf16), vv, preferred_element_type=f32)
                car_ref[h] = car + r_tot
            return c

        lax.fori_loop(0, qb, k_block, 0)
        out = jnp.where(head_mask[0], acc_ref[0], acc_ref[1])
        o_ref[pl.ds(q0, blk), :] = out.astype(o_ref.dtype)
        return carry

    lax.fori_loop(0, nqb, q_block, 0)


def _sb_attention(proj, batch, seq):
    t = batch * seq
    lanes = V7X_LANES
    n_pairs = SB_WIDTH // lanes
    qb0, kb0, vb0 = OFF_SBQ // lanes, OFF_SBK // lanes, OFF_SBV // lanes
    est = 2 * 4 * seq * lanes * 2 + 4 * SB_BLOCK * SB_BLOCK * 4 + 8 * 1024 * 1024
    return pl.pallas_call(
        functools.partial(_sb_kernel, seq=seq),
        grid=(batch, n_pairs),
        in_specs=[pl.BlockSpec((seq, lanes), lambda b, p: (b, qb0 + p)),
                  pl.BlockSpec((seq, lanes), lambda b, p: (b, kb0 + p)),
                  pl.BlockSpec((seq, lanes), lambda b, p: (b, vb0 + p))],
        out_specs=pl.BlockSpec((seq, lanes), lambda b, p: (b, p)),
        out_shape=jax.ShapeDtypeStruct((t, SB_WIDTH), bf16),
        scratch_shapes=[pltpu.VMEM((2, SB_BLOCK, SB_BLOCK), f32),
                        pltpu.VMEM((2, SB_BLOCK, SB_BLOCK), f32)],
        compiler_params=_vmem_params(est, ("arbitrary", "arbitrary")),
        name="sb_attention",
    )(proj, proj, proj)


def _route(scores, sel):
    c = [sel[:, e:e + 1] for e in range(N_EXPERTS)]
    s = [scores[:, e:e + 1] for e in range(N_EXPERTS)]
    group_score = []
    for g in range(N_GROUPS):
        a, b, cc, d = c[4 * g:4 * g + 4]
        hi1, lo1 = jnp.maximum(a, b), jnp.minimum(a, b)
        hi2, lo2 = jnp.maximum(cc, d), jnp.minimum(cc, d)
        top1 = jnp.maximum(hi1, hi2)
        top2 = jnp.maximum(jnp.minimum(hi1, hi2), jnp.maximum(lo1, lo2))
        group_score.append(top1 + top2)
    best = group_score[0]
    gidx = jnp.zeros_like(best, dtype=i32)
    for g in range(1, N_GROUPS):
        better = group_score[g] > best
        gidx = jnp.where(better, g, gidx)
        best = jnp.where(better, group_score[g], best)
    vals, scs = [], []
    for j in range(EXPERTS_PER_GROUP):
        vj, sj = c[j], s[j]
        for g in range(1, N_GROUPS):
            pick = gidx == g
            vj = jnp.where(pick, c[4 * g + j], vj)
            sj = jnp.where(pick, s[4 * g + j], sj)
        vals.append(vj)
        scs.append(sj)
    m1, i1, w1 = vals[0], jnp.zeros_like(gidx), scs[0]
    for j in range(1, EXPERTS_PER_GROUP):
        better = vals[j] > m1
        m1 = jnp.where(better, vals[j], m1)
        i1 = jnp.where(better, j, i1)
        w1 = jnp.where(better, scs[j], w1)
    m2 = jnp.full_like(m1, -jnp.inf)
    i2 = jnp.full_like(i1, -1)
    w2 = jnp.zeros_like(w1)
    for j in range(EXPERTS_PER_GROUP):
        better = (i1 != j) & ((vals[j] > m2) | (i2 < 0))
        m2 = jnp.where(better, vals[j], m2)
        i2 = jnp.where(better, j, i2)
        w2 = jnp.where(better, scs[j], w2)
    denom = w1 + w2
    return w1 / denom, w2 / denom, gidx * EXPERTS_PER_GROUP + i1, gidx * EXPERTS_PER_GROUP + i2


def _mix_kernel(proj_ref, ysb_ref, x_ref, convw_ref, convb_ref, wg2_ref, bg_ref, normw_ref,
                wout_ref, lng_ref, lnb_ref, rw_ref, rb_ref,
                x1r_ref, rinfo_ref,
                halo_ref, state_ref, ogla_ref, *, ts, alpha):
    t_idx = pl.program_id(1)
    kp = GLA_KEY_PAD
    ck = GLA_CHUNK

    @pl.when(t_idx == 0)
    def _():
        halo_ref[...] = jnp.zeros_like(halo_ref)
        state_ref[...] = jnp.zeros_like(state_ref)

    c_h = proj_ref[:, OFF_CONV:OFF_CONV + CONV_WIDTH].astype(f32)
    c_b = proj_ref[:, OFF_CONV + CONV_WIDTH:OFF_CONV + 2 * CONV_WIDTH].astype(f32)
    c_c = proj_ref[:, OFF_CONV + 2 * CONV_WIDTH:OFF_CONV + 3 * CONV_WIDTH].astype(f32)
    u = c_c * c_h
    ucat = jnp.concatenate([halo_ref[...], u], axis=0)
    halo_ref[...] = u[ts - V7X_SUBLANES:, :]
    u1 = ucat[V7X_SUBLANES - 1:V7X_SUBLANES - 1 + ts, :]
    u2 = ucat[V7X_SUBLANES - 2:V7X_SUBLANES - 2 + ts, :]
    y = convb_ref[...] + convw_ref[0:1, :] * u
    y = y + convw_ref[1:2, :] * u1
    y = y + convw_ref[2:3, :] * u2
    y_conv = c_b * y

    lr = proj_ref[:, OFF_LR:OFF_LR + V7X_LANES]
    pre = jnp.dot(lr, wg2_ref[...], preferred_element_type=f32) + bg_ref[...]
    log_a = (jnp.minimum(pre, 0.0) - jnp.log1p(jnp.exp(_neg_abs(pre)))) * (1.0 / GLA_GATE_NORM)
    r64 = lax.broadcasted_iota(i32, (ck, ck), 0)
    c64 = lax.broadcasted_iota(i32, (ck, ck), 1)
    tril = c64 <= r64
    tril_b = jnp.where(tril, 1.0, 0.0).astype(bf16)
    tril2 = jnp.concatenate([tril_b, tril_b], axis=1)
    klane = lax.broadcasted_iota(i32, (ck, kp), 1)
    vlane = lax.broadcasted_iota(i32, (ck, GLA_WIDTH), 1)
    kmask = [(klane >= h * GLA_DK) & (klane < (h + 1) * GLA_DK) for h in range(GLA_HEADS)]
    vmask = [(vlane >= h * GLA_DV) & (vlane < (h + 1) * GLA_DV) for h in range(GLA_HEADS)]
    srow = lax.broadcasted_iota(i32, (GLA_WIDTH, kp), 0)
    scol = lax.broadcasted_iota(i32, (GLA_WIDTH, kp), 1)
    blockdiag = jnp.zeros((GLA_WIDTH, kp), jnp.bool_)
    for h in range(GLA_HEADS):
        blockdiag = blockdiag | ((srow >= h * GLA_DV) & (srow < (h + 1) * GLA_DV)
                                 & (scol >= h * GLA_DK) & (scol < (h + 1) * GLA_DK))
    q_scale = GLA_DK ** -0.5
    for ci in range(ts // ck):
        r0 = ci * ck
        la = log_a[r0:r0 + ck, :]
        la_hi = la.astype(bf16)
        la_lo = (la - la_hi.astype(f32)).astype(bf16)
        bcum = jnp.dot(tril2, jnp.concatenate([la_hi, la_lo], axis=0), preferred_element_type=f32)
        b_last = bcum[ck - 1:ck, :]
        qv = proj_ref[r0:r0 + ck, OFF_GQ:OFF_GQ + kp].astype(f32)
        kv_ = proj_ref[r0:r0 + ck, OFF_GK:OFF_GK + kp].astype(f32)
        vb = proj_ref[r0:r0 + ck, OFF_GV:OFF_GV + GLA_WIDTH]
        q_in = (qv * q_scale) * jnp.exp(bcum)
        k_in = (kv_ * jnp.exp(-bcum)).astype(bf16)
        k_te = (kv_ * jnp.exp(b_last - bcum)).astype(bf16)
        state = state_ref[...]
        o = lax.dot_general(q_in.astype(bf16), state.astype(bf16), (((1,), (1,)), ((), ())),
                            preferred_element_type=f32)
        o_intra = jnp.zeros((ck, GLA_WIDTH), f32)
        for h in range(GLA_HEADS):
            qh = jnp.where(kmask[h], q_in, 0.0).astype(bf16)
            sc = lax.dot_general(qh, k_in, (((1,), (1,)), ((), ())), preferred_element_type=f32)
            sc = jnp.where(tril, sc, 0.0).astype(bf16)
            oh = jnp.dot(sc, vb, preferred_element_type=f32)
            o_intra = jnp.where(vmask[h], oh, o_intra)
        ogla_ref[r0:r0 + ck, :] = o_intra + o
        kvt = lax.dot_general(vb, k_te, (((0,), (0,)), ((), ())), preferred_element_type=f32)
        state_ref[...] = jnp.exp(b_last) * state + jnp.where(blockdiag, kvt, 0.0)

    o_all = ogla_ref[...]
    o_sq = o_all * o_all
    vlane_t = lax.broadcasted_iota(i32, (ts, GLA_WIDTH), 1)
    inv = jnp.zeros((ts, GLA_WIDTH), f32)
    for h in range(GLA_HEADS):
        hm = (vlane_t >= h * GLA_DV) & (vlane_t < (h + 1) * GLA_DV)
        ms = jnp.sum(jnp.where(hm, o_sq, 0.0), axis=-1, keepdims=True) * (1.0 / GLA_DV)
        inv = jnp.where(hm, lax.rsqrt(ms + RMS_EPS), inv)
    gg = proj_ref[:, OFF_GG:OFF_GG + GLA_WIDTH].astype(f32)
    y_gla = (o_all * inv * normw_ref[...]) * (gg * (1.0 / (1.0 + jnp.exp(-gg))))

    mix = jnp.concatenate([y_conv.astype(bf16), ysb_ref[...], y_gla.astype(bf16)], axis=1)
    proj_out = jnp.dot(mix, wout_ref[...], preferred_element_type=f32)
    x1 = _layer_norm(alpha * x_ref[...] + proj_out, lng_ref[...], lnb_ref[...])
    _rows_to_tiles_store(x1r_ref, x1, ts)

    logits = jnp.dot(x1.astype(bf16), rw_ref[...], preferred_element_type=f32)
    scores = 1.0 / (1.0 + jnp.exp(-logits))
    sel = scores + rb_ref[...]
    g0, g1, e0, e1 = _route(scores, sel)
    lane = lax.broadcasted_iota(i32, (ts, V7X_LANES), 1)
    info = jnp.where(lane == 0, g0, 0.0)
    info = jnp.where(lane == 1, g1, info)
    info = jnp.where(lane == 2, e0.astype(f32), info)
    info = jnp.where(lane == 3, e1.astype(f32), info)
    rinfo_ref[...] = info


def _mix(proj, ysb, x2d, lw, batch, seq, alpha):
    t = batch * seq
    ts = 512
    nt = seq // ts
    row = lambda b, s: (b * nt + s, 0)
    const = lambda b, s: (0, 0)
    est = (2 * (ts * P_WIDTH * 2 + ts * SB_WIDTH * 2 + 2 * ts * D_MODEL * 4 + D_MODEL * D_MODEL * 2
                + D_MODEL * V7X_LANES * 2 + ts * V7X_LANES * 4) + 24 * 1024 * 1024)
    return pl.pallas_call(
        functools.partial(_mix_kernel, ts=ts, alpha=alpha),
        grid=(batch, nt),
        in_specs=[pl.BlockSpec((ts, P_WIDTH), row),
                  pl.BlockSpec((ts, SB_WIDTH), row),
                  pl.BlockSpec((ts, D_MODEL), row),
                  pl.BlockSpec((V7X_SUBLANES, CONV_WIDTH), const),
                  pl.BlockSpec((1, CONV_WIDTH), const),
                  pl.BlockSpec((V7X_LANES, GLA_KEY_PAD), const),
                  pl.BlockSpec((1, GLA_KEY_PAD), const),
                  pl.BlockSpec((1, GLA_WIDTH), const),
                  pl.BlockSpec((D_MODEL, D_MODEL), const),
                  pl.BlockSpec((1, D_MODEL), const),
                  pl.BlockSpec((1, D_MODEL), const),
                  pl.BlockSpec((D_MODEL, V7X_LANES), const),
                  pl.BlockSpec((1, V7X_LANES), const)],
        out_specs=[pl.BlockSpec((ts * ROW_TILE, V7X_LANES), row),
                   pl.BlockSpec((ts, V7X_LANES), row)],
        out_shape=[jax.ShapeDtypeStruct((t * ROW_TILE, V7X_LANES), f32),
                   jax.ShapeDtypeStruct((t, V7X_LANES), f32)],
        scratch_shapes=[pltpu.VMEM((V7X_SUBLANES, CONV_WIDTH), f32),
                        pltpu.VMEM((GLA_WIDTH, GLA_KEY_PAD), f32),
                        pltpu.VMEM((ts, GLA_WIDTH), f32)],
        compiler_params=_vmem_params(est, ("arbitrary", "arbitrary")),
        name="mix_ln_router",
    )(proj, ysb, x2d, lw["conv_w"], lw["conv_b"], lw["wg2"], lw["bg"], lw["normw"], lw["w_out"],
      lw["ln1_g"], lw["ln1_b"], lw["rw"], lw["rb"])


def _expert_kernel(texp_ref, tvalid_ref, src_ref, dst_ref, x1r_hbm, w1_ref, w3_ref, w2_ref, y2_hbm,
                   xbuf, obuf, sem_in, sem_out, *, tm):
    i = pl.program_id(0)

    def row_in(r):
        tok = src_ref[0, 0, r]
        return pltpu.make_async_copy(
            x1r_hbm.at[pl.ds(pl.multiple_of(tok * ROW_TILE, ROW_TILE), ROW_TILE), :],
            xbuf.at[pl.ds(pl.multiple_of(r * ROW_TILE, ROW_TILE), ROW_TILE), :], sem_in)

    def row_out(r):
        d = dst_ref[0, 0, r]
        return pltpu.make_async_copy(
            obuf.at[pl.ds(pl.multiple_of(r * ROW_TILE, ROW_TILE), ROW_TILE), :],
            y2_hbm.at[pl.ds(pl.multiple_of(d * ROW_TILE, ROW_TILE), ROW_TILE), :], sem_out)

    @pl.when(tvalid_ref[i] > 0)
    def _():
        def issue_in(r, c):
            row_in(r).start()
            return c
        lax.fori_loop(0, tm, issue_in, 0)

        def wait_in(r, c):
            row_in(r).wait()
            return c
        lax.fori_loop(0, tm, wait_in, 0)

        xb = _tiles_to_rows_load(xbuf, tm).astype(bf16)
        a = jnp.dot(xb, w1_ref[0], preferred_element_type=f32)
        b = jnp.dot(xb, w3_ref[0], preferred_element_type=f32)
        h = (a * (1.0 / (1.0 + jnp.exp(-a)))) * b
        o = jnp.dot(h.astype(bf16), w2_ref[0], preferred_element_type=f32)
        _rows_to_tiles_store(obuf, o, tm)

        def issue_out(r, c):
            row_out(r).start()
            return c
        lax.fori_loop(0, tm, issue_out, 0)

        def wait_out(r, c):
            row_out(r).wait()
            return c
        lax.fori_loop(0, tm, wait_out, 0)


def _experts(x1r, src, dst, tile_expert, tile_valid, w1b, w3b, w2b, layer, n_rows, tm):
    n_tiles = n_rows // tm
    base = layer * N_EXPERTS
    est = (2 * 3 * D_MODEL * D_EXPERT * 2 + 2 * tm * ROW_TILE * V7X_LANES * 4 + 16 * 1024 * 1024)
    grid_spec = pltpu.PrefetchScalarGridSpec(
        num_scalar_prefetch=2,
        grid=(n_tiles,),
        in_specs=[pl.BlockSpec((1, 1, tm), lambda i, te, tv: (i, 0, 0), memory_space=pltpu.SMEM),
                  pl.BlockSpec((1, 1, tm), lambda i, te, tv: (i, 0, 0), memory_space=pltpu.SMEM),
                  pl.BlockSpec(memory_space=pl.ANY),
                  pl.BlockSpec((1, D_MODEL, D_EXPERT), lambda i, te, tv: (base + te[i], 0, 0)),
                  pl.BlockSpec((1, D_MODEL, D_EXPERT), lambda i, te, tv: (base + te[i], 0, 0)),
                  pl.BlockSpec((1, D_EXPERT, D_MODEL), lambda i, te, tv: (base + te[i], 0, 0))],
        out_specs=pl.BlockSpec(memory_space=pl.ANY),
        scratch_shapes=[pltpu.VMEM((tm * ROW_TILE, V7X_LANES), f32),
                        pltpu.VMEM((tm * ROW_TILE, V7X_LANES), f32),
                        pltpu.SemaphoreType.DMA(()),
                        pltpu.SemaphoreType.DMA(())],
    )
    return pl.pallas_call(
        functools.partial(_expert_kernel, tm=tm),
        grid_spec=grid_spec,
        out_shape=jax.ShapeDtypeStruct((n_rows * ROW_TILE, V7X_LANES), f32),
        compiler_params=_vmem_params(est, ("arbitrary",)),
        name="experts",
    )(tile_expert, tile_valid, src.reshape(n_tiles, 1, tm), dst.reshape(n_tiles, 1, tm),
      x1r, w1b, w3b, w2b)


def _route_plan(rinfo, n_tok, tm):
    n_pick = 2 * n_tok
    n_rows = n_pick + N_EXPERTS * tm
    e = jnp.concatenate([rinfo[:, 2], rinfo[:, 3]]).astype(i32)
    onehot = (e[:, None] == jnp.arange(N_EXPERTS, dtype=i32)[None, :]).astype(i32)
    csum = jnp.cumsum(onehot, axis=0)
    rank = jnp.take_along_axis(csum, e[:, None], axis=1)[:, 0] - 1
    counts = csum[-1]
    padded = ((counts + tm - 1) // tm) * tm
    ends = jnp.cumsum(padded)
    starts = ends - padded
    dest = starts[e] + rank
    picks = jnp.arange(n_pick, dtype=i32)
    src = jnp.zeros((n_rows,), i32).at[dest].set(picks % n_tok)
    is_pad = jnp.ones((n_rows,), i32).at[dest].set(0)
    pad_rank = jnp.cumsum(is_pad) - 1
    dst = (n_pick + pad_rank).astype(i32).at[dest].set(picks)
    tile_start = jnp.arange(n_rows // tm, dtype=i32) * tm
    tile_expert = jnp.minimum(jnp.searchsorted(ends, tile_start, side="right"), N_EXPERTS - 1).astype(i32)
    tile_valid = (tile_start < ends[-1]).astype(i32)
    return src, dst, tile_expert, tile_valid, n_rows


def _combine_kernel(y0_ref, y1_ref, x1r_ref, rinfo_ref, g_ref, b_ref, o_ref, *, tm, alpha):
    o0 = _tiles_to_rows_load(y0_ref, tm)
    o1 = _tiles_to_rows_load(y1_ref, tm)
    x1 = _tiles_to_rows_load(x1r_ref, tm)
    g0 = rinfo_ref[:, 0:1]
    g1 = rinfo_ref[:, 1:2]
    ffn = g0 * o0 + g1 * o1
    o_ref[...] = _layer_norm(alpha * x1 + ffn, g_ref[...], b_ref[...])


def _combine(y2, x1r, rinfo, ln_g, ln_b, n_tok, alpha):
    tm = 512
    nb = n_tok // tm
    blk = (tm * ROW_TILE, V7X_LANES)
    est = 2 * (3 * tm * D_MODEL * 4 + tm * V7X_LANES * 4 + tm * D_MODEL * 4) + 16 * 1024 * 1024
    return pl.pallas_call(
        functools.partial(_combine_kernel, tm=tm, alpha=alpha),
        grid=(nb,),
        in_specs=[pl.BlockSpec(blk, lambda i: (i, 0)),
                  pl.BlockSpec(blk, lambda i: (nb + i, 0)),
                  pl.BlockSpec(blk, lambda i: (i, 0)),
                  pl.BlockSpec((tm, V7X_LANES), lambda i: (i, 0)),
                  pl.BlockSpec((1, D_MODEL), lambda i: (0, 0)),
                  pl.BlockSpec((1, D_MODEL), lambda i: (0, 0))],
        out_specs=pl.BlockSpec((tm, D_MODEL), lambda i: (i, 0)),
        out_shape=jax.ShapeDtypeStruct((n_tok, D_MODEL), f32),
        compiler_params=_vmem_params(est, ("arbitrary",)),
        name="combine_ln",
    )(y2, y2, x1r, rinfo, ln_g, ln_b)


def _prep_w_in(w_in_l):
    conv = w_in_l[:, :OFF_SBQ]
    o = OFF_SBQ
    sbq = w_in_l[:, o:o + SB_WIDTH] * (SB_HEAD_DIM ** -0.5 * LOG2E)
    sbk = w_in_l[:, o + SB_WIDTH:o + 2 * SB_WIDTH]
    sbv = w_in_l[:, o + 2 * SB_WIDTH:o + 3 * SB_WIDTH]
    o = o + 3 * SB_WIDTH
    gq = w_in_l[:, o:o + GLA_KEY_WIDTH]
    gk = w_in_l[:, o + GLA_KEY_WIDTH:o + 2 * GLA_KEY_WIDTH]
    o = o + 2 * GLA_KEY_WIDTH
    gv = w_in_l[:, o:o + GLA_WIDTH]
    gg = w_in_l[:, o + GLA_WIDTH:o + 2 * GLA_WIDTH]
    lr = w_in_l[:, o + 2 * GLA_WIDTH:]
    zk = jnp.zeros((D_MODEL, GLA_KEY_PAD - GLA_KEY_WIDTH), f32)
    zl = jnp.zeros((D_MODEL, V7X_LANES - GLA_GATE_RANK), f32)
    return jnp.concatenate([conv, sbq, sbk, sbv, gq, zk, gk, zk, gv, gg, lr, zl], axis=1).astype(bf16)


def kernel(x, w_in, conv_w, conv_b, gla_w_gate2, gla_b_gate, gla_norm_w, w_out, ln1_g, ln1_b,
           router_w, router_bias, w1, w3, w2, ln2_g, ln2_b):
    batch, seq, d = x.shape
    depth = w_in.shape[0]
    n_tok = batch * seq
    alpha = (2 * depth) ** 0.25
    tm_e = 512

    w1b = w1.reshape(depth * N_EXPERTS, D_MODEL, D_EXPERT).astype(bf16)
    w3b = w3.reshape(depth * N_EXPERTS, D_MODEL, D_EXPERT).astype(bf16)
    w2b = w2.reshape(depth * N_EXPERTS, D_EXPERT, D_MODEL).astype(bf16)
    rw = jnp.pad(router_w, ((0, 0), (0, V7X_LANES - N_EXPERTS))).astype(bf16)
    rb = jnp.pad(router_bias.reshape(1, N_EXPERTS), ((0, 0), (0, V7X_LANES - N_EXPERTS)))

    x2d = x.reshape(n_tok, d)
    for l in range(depth):
        lw = dict(
            conv_w=jnp.pad(conv_w[l], ((0, V7X_SUBLANES - CONV_K), (0, 0))),
            conv_b=conv_b[l].reshape(1, CONV_WIDTH),
            wg2=jnp.pad(gla_w_gate2[l], ((0, V7X_LANES - GLA_GATE_RANK),
                                         (0, GLA_KEY_PAD - GLA_KEY_WIDTH))).astype(bf16),
            bg=jnp.pad(gla_b_gate[l].reshape(1, GLA_KEY_WIDTH), ((0, 0), (0, GLA_KEY_PAD - GLA_KEY_WIDTH))),
            normw=jnp.tile(gla_norm_w[l], GLA_HEADS).reshape(1, GLA_WIDTH),
            w_out=w_out[l].astype(bf16),
            ln1_g=ln1_g[l].reshape(1, d), ln1_b=ln1_b[l].reshape(1, d),
            rw=rw, rb=rb)
        proj = _in_proj(x2d, _prep_w_in(w_in[l]))
        ysb = _sb_attention(proj, batch, seq)
        x1r, rinfo = _mix(proj, ysb, x2d, lw, batch, seq, alpha)
        src, dst, tile_expert, tile_valid, n_rows = _route_plan(rinfo, n_tok, tm_e)
        y2 = _experts(x1r, src, dst, tile_expert, tile_valid, w1b, w3b, w2b, l, n_rows, tm_e)
        x2d = _combine(y2, x1r, rinfo, ln2_g[l].reshape(1, d), ln2_b[l].reshape(1, d), n_tok, alpha)
    return x2d.reshape(batch, seq, d)
```

```python
import functools

import jax
import jax.numpy as jnp
from jax import lax
from jax.experimental import pallas as pl
from jax.experimental.pallas import tpu as pltpu

f32 = jnp.float32
bf16 = jnp.bfloat16
i32 = jnp.int32

V7X_LANES = 128
V7X_SUBLANES = 8
V7X_VMEM_BYTES = 64 * 1024 * 1024

D_MODEL = 1024
CONV_WIDTH = 256
CONV_K = 3
SB_HEADS = 6
SB_HEAD_DIM = 64
SB_WIDTH = SB_HEADS * SB_HEAD_DIM
SB_BLOCK = 128
SB_QT = 256
GLA_HEADS = 4
GLA_DV = 96
GLA_DK = 48
GLA_WIDTH = GLA_HEADS * GLA_DV
GLA_KEY_WIDTH = GLA_HEADS * GLA_DK
GLA_KEY_PAD = 256
GLA_GATE_RANK = 16
GLA_GATE_NORM = 16.0
GLA_CHUNK = 64
N_EXPERTS = 16
N_GROUPS = 4
EXPERTS_PER_GROUP = 4
D_EXPERT = 512
LN_EPS = 1e-5
RMS_EPS = 1e-6
LOG2E = 1.4426950408889634

OFF_CONV = 0
OFF_SBQ = 3 * CONV_WIDTH
OFF_SBK = OFF_SBQ + SB_WIDTH
OFF_SBV = OFF_SBK + SB_WIDTH
OFF_GQ = OFF_SBV + SB_WIDTH
OFF_GK = OFF_GQ + GLA_KEY_PAD
OFF_GV = OFF_GK + GLA_KEY_PAD
OFF_GG = OFF_GV + GLA_WIDTH
OFF_LR = OFF_GG + GLA_WIDTH
P_WIDTH = OFF_LR + V7X_LANES

ROW_TILE = V7X_SUBLANES
MIX_TILE = 512
MOE_TOKEN_TILE = 512
MOE_ROW_TILE = 512
DMA_UNROLL = 8
INFO_ROWS = V7X_SUBLANES


def _vmem_params(est_bytes, semantics):
    limit = int(min(V7X_VMEM_BYTES - 6 * 1024 * 1024, max(est_bytes, 16 * 1024 * 1024)))
    return pltpu.CompilerParams(dimension_semantics=semantics, vmem_limit_bytes=limit)


def _neg_abs(x):
    bits = lax.bitcast_convert_type(x, jnp.uint32) | jnp.uint32(0x80000000)
    return lax.bitcast_convert_type(bits, f32)


def _rows_to_tiles_store(ref, val, n_rows):
    for j in range(D_MODEL // V7X_LANES):
        ref[pl.ds(j, n_rows, stride=ROW_TILE), :] = val[:, j * V7X_LANES:(j + 1) * V7X_LANES]


def _tiles_to_rows_load(ref, n_rows):
    return jnp.concatenate(
        [ref[pl.ds(j, n_rows, stride=ROW_TILE), :] for j in range(D_MODEL // V7X_LANES)], axis=1)


def _layer_norm(h, g, b):
    mu = jnp.mean(h, axis=-1, keepdims=True)
    c = h - mu
    var = jnp.mean(c * c, axis=-1, keepdims=True)
    return c * lax.rsqrt(var + LN_EPS) * g + b


def _in_proj_kernel(x_ref, w_ref, o_ref):
    xb = x_ref[...].astype(bf16)
    n_chunk = 256
    for n0 in range(0, P_WIDTH, n_chunk):
        o_ref[:, n0:n0 + n_chunk] = jnp.dot(
            xb, w_ref[:, n0:n0 + n_chunk], preferred_element_type=f32).astype(bf16)


def _in_proj(x2d, w):
    t = x2d.shape[0]
    tm = 512
    est = 2 * (tm * D_MODEL * 4 + D_MODEL * P_WIDTH * 2 + tm * P_WIDTH * 2) + 8 * 1024 * 1024
    return pl.pallas_call(
        _in_proj_kernel,
        grid=(t // tm,),
        in_specs=[pl.BlockSpec((tm, D_MODEL), lambda i: (i, 0)),
                  pl.BlockSpec((D_MODEL, P_WIDTH), lambda i: (0, 0))],
        out_specs=pl.BlockSpec((tm, P_WIDTH), lambda i: (i, 0)),
        out_shape=jax.ShapeDtypeStruct((t, P_WIDTH), bf16),
        compiler_params=_vmem_params(est, ("arbitrary",)),
        name="in_proj",
    )(x2d, w)


def _sb_kernel(q_ref, k_ref, v_ref, o_ref, acc_ref, car_ref, *, seq):
    blk = SB_BLOCK
    qt = SB_QT
    n_pairs = SB_WIDTH // V7X_LANES
    klane = lax.broadcasted_iota(i32, (blk, V7X_LANES), 1)
    head_mask = [klane < SB_HEAD_DIM, klane >= SB_HEAD_DIM]
    wr = lax.broadcasted_iota(i32, (2 * blk, 2 * blk), 0)
    wc = lax.broadcasted_iota(i32, (2 * blk, 2 * blk), 1)
    suffix_w = jnp.where(((wr >= blk) == (wc >= blk)) & (wr >= wc), 1.0, 0.0).astype(bf16)
    ri = lax.broadcasted_iota(i32, (qt, 4 * blk), 0)
    ci = lax.broadcasted_iota(i32, (qt, 4 * blk), 1)
    diag_valid = ((ci & (blk - 1)) + jnp.where(ci >= 2 * blk, blk, 0)) < ri

    def softplus2(z):
        return jnp.maximum(z, 0.0) + jnp.log2(1.0 + jnp.exp2(_neg_abs(z)))

    def lane_bcast(tot):
        return jnp.concatenate([jnp.broadcast_to(tot[:, 0:1], (qt, blk)),
                                jnp.broadcast_to(tot[:, blk:blk + 1], (qt, blk))], axis=1)

    def span(q_tiles, k0, masked):
        k_a = k_ref[pl.ds(k0, blk), :]
        k_b = k_ref[pl.ds(k0 + blk, blk), :]
        v_a = v_ref[pl.ds(k0, blk), :]
        v_b = v_ref[pl.ds(k0 + blk, blk), :]
        for p in range(n_pairs):
            sl = slice(p * V7X_LANES, (p + 1) * V7X_LANES)

            def stack(x_a, x_b):
                zero = jnp.zeros((blk, V7X_LANES), bf16)
                return jnp.concatenate([jnp.where(head_mask[0], x_a[:, sl], zero),
                                        jnp.where(head_mask[1], x_a[:, sl], zero),
                                        jnp.where(head_mask[0], x_b[:, sl], zero),
                                        jnp.where(head_mask[1], x_b[:, sl], zero)], axis=0)

            z = lax.dot_general(q_tiles[p], stack(k_a, k_b), (((1,), (1,)), ((), ())),
                                preferred_element_type=f32)
            sp = softplus2(z)
            if masked:
                sp = jnp.where(diag_valid, sp, 0.0)
            spb = sp.astype(bf16)
            tot_b = jnp.dot(spb[:, 2 * blk:], suffix_w, preferred_element_type=f32)
            if not masked:
                tot_b = tot_b + car_ref[p]
            tot_a = jnp.dot(spb[:, :2 * blk], suffix_w, preferred_element_type=f32) + lane_bcast(tot_b)
            car_ref[p] = lane_bcast(tot_a)
            a = jnp.exp2(z - jnp.concatenate([tot_a, tot_b], axis=1))
            if masked:
                a = jnp.where(diag_valid, a, 0.0)
            av = jnp.dot(a.astype(bf16), stack(v_a, v_b), preferred_element_type=f32)
            if masked:
                acc_ref[p] = av
            else:
                acc_ref[p] += av

    def q_tile(qi, carry):
        q0 = pl.multiple_of(qi * qt, qt)
        q_all = q_ref[pl.ds(q0, qt), :]
        q_tiles = [q_all[:, p * V7X_LANES:(p + 1) * V7X_LANES] for p in range(n_pairs)]
        span(q_tiles, q0, True)

        def k_span(i, c):
            span(q_tiles, pl.multiple_of((qi - 1 - i) * qt, qt), False)
            return c

        lax.fori_loop(0, qi, k_span, 0)
        o_ref[pl.ds(q0, qt), :] = jnp.concatenate(
            [acc_ref[p] for p in range(n_pairs)], axis=1).astype(o_ref.dtype)
        return carry

    lax.fori_loop(0, seq // qt, q_tile, 0)


def _sb_attention(proj, batch, seq):
    t = batch * seq
    n_pairs = SB_WIDTH // V7X_LANES
    est = 2 * 4 * seq * SB_WIDTH * 2 + n_pairs * SB_QT * 3 * V7X_LANES * 4 + 24 * 1024 * 1024
    return pl.pallas_call(
        functools.partial(_sb_kernel, seq=seq),
        grid=(batch,),
        in_specs=[pl.BlockSpec((seq, SB_WIDTH), lambda b: (b, OFF_SBQ // SB_WIDTH)),
                  pl.BlockSpec((seq, SB_WIDTH), lambda b: (b, OFF_SBK // SB_WIDTH)),
                  pl.BlockSpec((seq, SB_WIDTH), lambda b: (b, OFF_SBV // SB_WIDTH))],
        out_specs=pl.BlockSpec((seq, SB_WIDTH), lambda b: (b, 0)),
        out_shape=jax.ShapeDtypeStruct((t, SB_WIDTH), bf16),
        scratch_shapes=[pltpu.VMEM((n_pairs, SB_QT, V7X_LANES), f32),
                        pltpu.VMEM((n_pairs, SB_QT, 2 * SB_BLOCK), f32)],
        compiler_params=_vmem_params(est, ("arbitrary",)),
        name="sb_attention",
    )(proj, proj, proj)


def _route(sel, scores):
    c, s = sel, scores
    group_score = []
    for g in range(N_GROUPS):
        a, b, cc, d = c[4 * g:4 * g + 4]
        hi1, lo1 = jnp.maximum(a, b), jnp.minimum(a, b)
        hi2, lo2 = jnp.maximum(cc, d), jnp.minimum(cc, d)
        top1 = jnp.maximum(hi1, hi2)
        top2 = jnp.maximum(jnp.minimum(hi1, hi2), jnp.maximum(lo1, lo2))
        group_score.append(top1 + top2)
    best = group_score[0]
    gidx = jnp.zeros_like(best, dtype=i32)
    for g in range(1, N_GROUPS):
        better = group_score[g] > best
        gidx = jnp.where(better, g, gidx)
        best = jnp.where(better, group_score[g], best)
    vals, scs = [], []
    for j in range(EXPERTS_PER_GROUP):
        vj, sj = c[j], s[j]
        for g in range(1, N_GROUPS):
            pick = gidx == g
            vj = jnp.where(pick, c[4 * g + j], vj)
            sj = jnp.where(pick, s[4 * g + j], sj)
        vals.append(vj)
        scs.append(sj)
    m1, i1, w1 = vals[0], jnp.zeros_like(gidx), scs[0]
    for j in range(1, EXPERTS_PER_GROUP):
        better = vals[j] > m1
        m1 = jnp.where(better, vals[j], m1)
        i1 = jnp.where(better, j, i1)
        w1 = jnp.where(better, scs[j], w1)
    m2 = jnp.full_like(m1, -jnp.inf)
    i2 = jnp.full_like(i1, -1)
    w2 = jnp.zeros_like(w1)
    for j in range(EXPERTS_PER_GROUP):
        better = (i1 != j) & ((vals[j] > m2) | (i2 < 0))
        m2 = jnp.where(better, vals[j], m2)
        i2 = jnp.where(better, j, i2)
        w2 = jnp.where(better, scs[j], w2)
    denom = w1 + w2
    return w1 / denom, w2 / denom, gidx * EXPERTS_PER_GROUP + i1, gidx * EXPERTS_PER_GROUP + i2


def _mix_kernel(proj_ref, ysb_ref, x_ref, convw_ref, convb_ref, wg2_ref, bg_ref, normw_ref,
                wout_ref, lng_ref, lnb_ref, rwt_ref, rbt_ref,
                x1r_ref, rinfo_ref, counts_ref,
                halo_ref, state_ref, cnt_ref, *, ts, alpha):
    t_idx = pl.program_id(1)
    kp = GLA_KEY_PAD
    ck = GLA_CHUNK

    @pl.when((pl.program_id(0) == 0) & (t_idx == 0))
    def _():
        cnt_ref[...] = jnp.zeros_like(cnt_ref)

    @pl.when(t_idx == 0)
    def _():
        halo_ref[...] = jnp.zeros_like(halo_ref)
        state_ref[...] = jnp.zeros_like(state_ref)

    c_h = proj_ref[:, OFF_CONV:OFF_CONV + CONV_WIDTH].astype(f32)
    c_b = proj_ref[:, OFF_CONV + CONV_WIDTH:OFF_CONV + 2 * CONV_WIDTH].astype(f32)
    c_c = proj_ref[:, OFF_CONV + 2 * CONV_WIDTH:OFF_CONV + 3 * CONV_WIDTH].astype(f32)
    u = c_c * c_h
    ucat = jnp.concatenate([halo_ref[...], u], axis=0)
    halo_ref[...] = u[ts - V7X_SUBLANES:, :]
    u1 = ucat[V7X_SUBLANES - 1:V7X_SUBLANES - 1 + ts, :]
    u2 = ucat[V7X_SUBLANES - 2:V7X_SUBLANES - 2 + ts, :]
    y = convb_ref[...] + convw_ref[0:1, :] * u
    y = y + convw_ref[1:2, :] * u1
    y = y + convw_ref[2:3, :] * u2
    y_conv = c_b * y

    lr = proj_ref[:, OFF_LR:OFF_LR + V7X_LANES]
    pre = jnp.dot(lr, wg2_ref[...], preferred_element_type=f32) + bg_ref[...]
    log_a = (jnp.minimum(pre, 0.0) - jnp.log1p(jnp.exp(_neg_abs(pre)))) * (1.0 / GLA_GATE_NORM)
    rr = lax.broadcasted_iota(i32, (ts, ts), 0)
    cc = lax.broadcasted_iota(i32, (ts, ts), 1)
    chunk_shift = ck.bit_length() - 1
    same_chunk = (rr >> chunk_shift) == (cc >> chunk_shift)
    cum_w = jnp.where(same_chunk & (cc <= rr), 1.0, 0.0).astype(bf16)
    tot_w = jnp.where(same_chunk, 1.0, 0.0).astype(bf16)
    la_hi = log_a.astype(bf16)
    la_lo = (log_a - la_hi.astype(f32)).astype(bf16)
    la2 = jnp.concatenate([la_hi, la_lo], axis=0)
    bcum = jnp.dot(jnp.concatenate([cum_w, cum_w], axis=1), la2, preferred_element_type=f32)
    blast = jnp.dot(jnp.concatenate([tot_w, tot_w], axis=1), la2, preferred_element_type=f32)
    qf = proj_ref[:, OFF_GQ:OFF_GQ + kp].astype(f32)
    kf = proj_ref[:, OFF_GK:OFF_GK + kp].astype(f32)
    vb = proj_ref[:, OFF_GV:OFF_GV + GLA_WIDTH]
    q_in = ((qf * (GLA_DK ** -0.5)) * jnp.exp(bcum)).astype(bf16)
    k_in = (kf * jnp.exp(-bcum)).astype(bf16)
    k_te = (kf * jnp.exp(blast - bcum)).astype(bf16)
    decay = jnp.exp(blast)

    dck = 2 * ck
    klane = lax.broadcasted_iota(i32, (dck, kp), 1)
    vlane = lax.broadcasted_iota(i32, (dck, GLA_WIDTH), 1)
    kmask = [(klane >= h * GLA_DK) & (klane < (h + 1) * GLA_DK) for h in range(GLA_HEADS)]
    vmask = [(vlane >= h * GLA_DV) & (vlane < (h + 1) * GLA_DV) for h in range(GLA_HEADS)]
    ir = lax.broadcasted_iota(i32, (dck, dck), 0)
    ic = lax.broadcasted_iota(i32, (dck, dck), 1)
    intra_mask = ((ir >> chunk_shift) == (ic >> chunk_shift)) & (ic <= ir)
    o_intra = []
    for j in range(ts // dck):
        rows = slice(j * dck, (j + 1) * dck)
        q_j, k_j, v_j = q_in[rows], k_in[rows], vb[rows]
        o_j = jnp.zeros((dck, GLA_WIDTH), f32)
        for h in range(GLA_HEADS):
            qh = jnp.where(kmask[h], q_j, jnp.zeros_like(q_j))
            sc = lax.dot_general(qh, k_j, (((1,), (1,)), ((), ())), preferred_element_type=f32)
            sc = jnp.where(intra_mask, sc, 0.0).astype(bf16)
            o_j = jnp.where(vmask[h], jnp.dot(sc, v_j, preferred_element_type=f32), o_j)
        o_intra.append(o_j)
    o_intra = jnp.concatenate(o_intra, axis=0)

    srow = lax.broadcasted_iota(i32, (GLA_WIDTH, kp), 0)
    scol = lax.broadcasted_iota(i32, (GLA_WIDTH, kp), 1)
    blockdiag = jnp.zeros((GLA_WIDTH, kp), jnp.bool_)
    for h in range(GLA_HEADS):
        blockdiag = blockdiag | ((srow >= h * GLA_DV) & (srow < (h + 1) * GLA_DV)
                                 & (scol >= h * GLA_DK) & (scol < (h + 1) * GLA_DK))
    kvt = [lax.dot_general(vb[c * ck:(c + 1) * ck], k_te[c * ck:(c + 1) * ck], (((0,), (0,)), ((), ())),
                           preferred_element_type=f32) for c in range(ts // ck)]
    state = state_ref[...]
    o_inter = []
    for c in range(ts // ck):
        o_inter.append(lax.dot_general(q_in[c * ck:(c + 1) * ck], state.astype(bf16),
                                       (((1,), (1,)), ((), ())), preferred_element_type=f32))
        state = decay[c * ck:c * ck + 1, :] * state + jnp.where(blockdiag, kvt[c], 0.0)
    state_ref[...] = state
    o_all = o_intra + jnp.concatenate(o_inter, axis=0)

    o_sq = o_all * o_all
    vlane_t = lax.broadcasted_iota(i32, (ts, GLA_WIDTH), 1)
    inv = jnp.zeros((ts, GLA_WIDTH), f32)
    for h in range(GLA_HEADS):
        hm = (vlane_t >= h * GLA_DV) & (vlane_t < (h + 1) * GLA_DV)
        ms = jnp.sum(jnp.where(hm, o_sq, 0.0), axis=-1, keepdims=True) * (1.0 / GLA_DV)
        inv = jnp.where(hm, lax.rsqrt(ms + RMS_EPS), inv)
    gg = proj_ref[:, OFF_GG:OFF_GG + GLA_WIDTH].astype(f32)
    y_gla = (o_all * inv * normw_ref[...]) * (gg * (1.0 / (1.0 + jnp.exp(-gg))))

    mix = jnp.concatenate([y_conv.astype(bf16), ysb_ref[...], y_gla.astype(bf16)], axis=1)
    proj_out = jnp.dot(mix, wout_ref[...], preferred_element_type=f32)
    x1 = _layer_norm(alpha * x_ref[...] + proj_out, lng_ref[...], lnb_ref[...])
    _rows_to_tiles_store(x1r_ref, x1, ts)

    logits = lax.dot_general(rwt_ref[...], x1.astype(bf16), (((1,), (1,)), ((), ())),
                             preferred_element_type=f32)
    scores = 1.0 / (1.0 + jnp.exp(-logits))
    sel = scores + rbt_ref[...]
    g0, g1, e0, e1 = _route([sel[e:e + 1, :] for e in range(N_EXPERTS)],
                            [scores[e:e + 1, :] for e in range(N_EXPERTS)])
    eid = lax.broadcasted_iota(i32, (N_EXPERTS, ts), 0)
    pick0 = eid == e0
    pick1 = eid == e1
    onehot = jnp.where(pick0 | pick1, 1.0, 0.0)
    before = jnp.dot(onehot.astype(bf16), jnp.where(rr < cc, 1.0, 0.0).astype(bf16),
                     preferred_element_type=f32) + cnt_ref[...]
    rank0 = jnp.sum(jnp.where(pick0, before, 0.0), axis=0, keepdims=True)
    rank1 = jnp.sum(jnp.where(pick1, before, 0.0), axis=0, keepdims=True)
    new_cnt = cnt_ref[...] + jnp.sum(onehot, axis=1, keepdims=True)
    cnt_ref[...] = new_cnt
    counts_ref[...] = new_cnt[:, :V7X_LANES]
    zero = jnp.zeros_like(g0)
    rinfo_ref[...] = jnp.concatenate(
        [g0, g1, e0.astype(f32), e1.astype(f32), rank0, rank1, zero, zero], axis=0)


def _mix(proj, ysb, x2d, lw, batch, seq, alpha):
    t = batch * seq
    ts = MIX_TILE
    nt = seq // ts
    row = lambda b, s: (b * nt + s, 0)
    const = lambda b, s: (0, 0)
    est = (2 * (ts * P_WIDTH * 2 + ts * SB_WIDTH * 2 + 2 * ts * D_MODEL * 4 + D_MODEL * D_MODEL * 2
                + ts * V7X_LANES * 4) + 28 * 1024 * 1024)
    return pl.pallas_call(
        functools.partial(_mix_kernel, ts=ts, alpha=alpha),
        grid=(batch, nt),
        in_specs=[pl.BlockSpec((ts, P_WIDTH), row),
                  pl.BlockSpec((ts, SB_WIDTH), row),
                  pl.BlockSpec((ts, D_MODEL), row),
                  pl.BlockSpec((V7X_SUBLANES, CONV_WIDTH), const),
                  pl.BlockSpec((1, CONV_WIDTH), const),
                  pl.BlockSpec((V7X_LANES, GLA_KEY_PAD), const),
                  pl.BlockSpec((1, GLA_KEY_PAD), const),
                  pl.BlockSpec((1, GLA_WIDTH), const),
                  pl.BlockSpec((D_MODEL, D_MODEL), const),
                  pl.BlockSpec((1, D_MODEL), const),
                  pl.BlockSpec((1, D_MODEL), const),
                  pl.BlockSpec((N_EXPERTS, D_MODEL), const),
                  pl.BlockSpec((N_EXPERTS, ts), const)],
        out_specs=[pl.BlockSpec((ts * ROW_TILE, V7X_LANES), row),
                   pl.BlockSpec((INFO_ROWS, ts), lambda b, s: (0, b * nt + s)),
                   pl.BlockSpec((N_EXPERTS, V7X_LANES), const)],
        out_shape=[jax.ShapeDtypeStruct((t * ROW_TILE, V7X_LANES), f32),
                   jax.ShapeDtypeStruct((INFO_ROWS, t), f32),
                   jax.ShapeDtypeStruct((N_EXPERTS, V7X_LANES), f32)],
        scratch_shapes=[pltpu.VMEM((V7X_SUBLANES, CONV_WIDTH), f32),
                        pltpu.VMEM((GLA_WIDTH, GLA_KEY_PAD), f32),
                        pltpu.VMEM((N_EXPERTS, ts), f32)],
        compiler_params=_vmem_params(est, ("arbitrary", "arbitrary")),
        name="mix_ln_router",
    )(proj, ysb, x2d, lw["conv_w"], lw["conv_b"], lw["wg2"], lw["bg"], lw["normw"], lw["w_out"],
      lw["ln1_g"], lw["ln1_b"], lw["rwt"], lw["rbt"])


def _dest_plan(rinfo, counts, n_tok):
    tm = MOE_ROW_TILE
    n_rows = 2 * n_tok + N_EXPERTS * tm
    cnt = counts[:, 0].astype(i32)
    padded = ((cnt + tm - 1) // tm) * tm
    ends = jnp.cumsum(padded)
    starts = ends - padded
    ids = jnp.arange(N_EXPERTS, dtype=i32)[None, :]
    e0 = rinfo[2].astype(i32)
    e1 = rinfo[3].astype(i32)
    dest0 = rinfo[4].astype(i32) + jnp.sum(jnp.where(e0[:, None] == ids, starts[None, :], 0), axis=1)
    dest1 = rinfo[5].astype(i32) + jnp.sum(jnp.where(e1[:, None] == ids, starts[None, :], 0), axis=1)
    nb = n_tok // MOE_TOKEN_TILE
    dest = jnp.concatenate([dest0.reshape(nb, 1, MOE_TOKEN_TILE), dest1.reshape(nb, 1, MOE_TOKEN_TILE)], axis=2)
    tile_start = jnp.arange(n_rows // tm, dtype=i32) * tm
    tile_expert = jnp.minimum(jnp.sum((tile_start[:, None] >= ends[None, :]).astype(i32), axis=1),
                              N_EXPERTS - 1).astype(i32)
    tile_valid = (tile_start < ends[-1]).astype(i32)
    tail = ends[-1] + jnp.arange(N_EXPERTS, dtype=i32) * tm
    zstart = jnp.concatenate([jnp.where(padded > 0, ends - tm, -1),
                              jnp.where(tail < n_rows, tail, -1)]).astype(i32)
    return dest, zstart, tile_expert, tile_valid, n_rows


def _dispatch_kernel(zstart_ref, dest_ref, x1r_ref, xs_hbm, zbuf, sem, zsem, *, tt):
    @pl.when(pl.program_id(0) == 0)
    def _():
        zbuf[...] = jnp.zeros_like(zbuf)

        def zero_copy(e):
            z0 = pl.multiple_of(jnp.maximum(zstart_ref[e], 0) * ROW_TILE, ROW_TILE)
            return pltpu.make_async_copy(zbuf, xs_hbm.at[pl.ds(z0, MOE_ROW_TILE * ROW_TILE), :], zsem)

        for e in range(2 * N_EXPERTS):
            @pl.when(zstart_ref[e] >= 0)
            def _():
                zero_copy(e).start()
        for e in range(2 * N_EXPERTS):
            @pl.when(zstart_ref[e] >= 0)
            def _():
                zero_copy(e).wait()

    def row_copy(r, j):
        d = dest_ref[0, 0, j]
        return pltpu.make_async_copy(
            x1r_ref.at[pl.ds(pl.multiple_of(r * ROW_TILE, ROW_TILE), ROW_TILE), :],
            xs_hbm.at[pl.ds(pl.multiple_of(d * ROW_TILE, ROW_TILE), ROW_TILE), :], sem)

    def issue(g, c):
        for u in range(DMA_UNROLL):
            r = g * DMA_UNROLL + u
            row_copy(r, r).start()
            row_copy(r, tt + r).start()
        return c

    lax.fori_loop(0, tt // DMA_UNROLL, issue, 0)
    whole = pltpu.make_async_copy(x1r_ref, xs_hbm.at[pl.ds(0, tt * ROW_TILE), :], sem)
    whole.wait()
    whole.wait()


def _dispatch(x1r, dest, zstart, n_tok, n_rows):
    tt = MOE_TOKEN_TILE
    est = 2 * tt * D_MODEL * 4 + MOE_ROW_TILE * D_MODEL * 4 + 8 * 1024 * 1024
    grid_spec = pltpu.PrefetchScalarGridSpec(
        num_scalar_prefetch=1,
        grid=(n_tok // tt,),
        in_specs=[pl.BlockSpec((1, 1, 2 * tt), lambda i, z: (i, 0, 0), memory_space=pltpu.SMEM),
                  pl.BlockSpec((tt * ROW_TILE, V7X_LANES), lambda i, z: (i, 0))],
        out_specs=pl.BlockSpec(memory_space=pl.ANY),
        scratch_shapes=[pltpu.VMEM((MOE_ROW_TILE * ROW_TILE, V7X_LANES), f32),
                        pltpu.SemaphoreType.DMA(()),
                        pltpu.SemaphoreType.DMA(())],
    )
    return pl.pallas_call(
        functools.partial(_dispatch_kernel, tt=tt),
        grid_spec=grid_spec,
        out_shape=jax.ShapeDtypeStruct((n_rows * ROW_TILE, V7X_LANES), f32),
        compiler_params=_vmem_params(est, ("arbitrary",)),
        name="dispatch",
    )(zstart, dest, x1r)


def _expert_kernel(texp_ref, tvalid_ref, xs_ref, w1_ref, w3_ref, w2_ref, ys_ref, *, tm):
    @pl.when(tvalid_ref[pl.program_id(0)] > 0)
    def _():
        xb = _tiles_to_rows_load(xs_ref, tm).astype(bf16)
        a = jnp.dot(xb, w1_ref[0], preferred_element_type=f32)
        b = jnp.dot(xb, w3_ref[0], preferred_element_type=f32)
        h = (a * (1.0 / (1.0 + jnp.exp(-a)))) * b
        o = jnp.dot(h.astype(bf16), w2_ref[0], preferred_element_type=f32)
        _rows_to_tiles_store(ys_ref, o, tm)

    @pl.when(tvalid_ref[pl.program_id(0)] == 0)
    def _():
        ys_ref[...] = jnp.zeros_like(ys_ref)


def _experts(xs, tile_expert, tile_valid, w1b, w3b, w2b, layer, n_rows):
    tm = MOE_ROW_TILE
    base = layer * N_EXPERTS
    blk = (tm * ROW_TILE, V7X_LANES)
    est = 2 * (3 * D_MODEL * D_EXPERT * 2 + 2 * tm * D_MODEL * 4) + 16 * 1024 * 1024
    grid_spec = pltpu.PrefetchScalarGridSpec(
        num_scalar_prefetch=2,
        grid=(n_rows // tm,),
        in_specs=[pl.BlockSpec(blk, lambda i, te, tv: (i, 0)),
                  pl.BlockSpec((1, D_MODEL, D_EXPERT), lambda i, te, tv: (base + te[i], 0, 0)),
                  pl.BlockSpec((1, D_MODEL, D_EXPERT), lambda i, te, tv: (base + te[i], 0, 0)),
                  pl.BlockSpec((1, D_EXPERT, D_MODEL), lambda i, te, tv: (base + te[i], 0, 0))],
        out_specs=pl.BlockSpec(blk, lambda i, te, tv: (i, 0)),
    )
    return pl.pallas_call(
        functools.partial(_expert_kernel, tm=tm),
        grid_spec=grid_spec,
        out_shape=jax.ShapeDtypeStruct((n_rows * ROW_TILE, V7X_LANES), f32),
        compiler_params=_vmem_params(est, ("arbitrary",)),
        name="experts",
    )(tile_expert, tile_valid, xs, w1b, w3b, w2b)


def _combine_kernel(dest_ref, dest_next_ref, ys_hbm, x1r_ref, rinfo_ref, g_ref, b_ref, o_ref,
                    buf, sem, *, tt, alpha):
    i = pl.program_id(0)
    n = pl.num_programs(0)
    slot = lax.rem(i, 2)

    def issue_all(d_ref, s):
        def issue(g, c):
            for u in range(DMA_UNROLL):
                j = g * DMA_UNROLL + u
                d = d_ref[0, 0, j]
                pltpu.make_async_copy(
                    ys_hbm.at[pl.ds(pl.multiple_of(d * ROW_TILE, ROW_TILE), ROW_TILE), :],
                    buf.at[s, pl.ds(pl.multiple_of(j * ROW_TILE, ROW_TILE), ROW_TILE), :],
                    sem.at[s]).start()
            return c
        lax.fori_loop(0, 2 * tt // DMA_UNROLL, issue, 0)

    @pl.when(i == 0)
    def _():
        issue_all(dest_ref, 0)

    @pl.when(i + 1 < n)
    def _():
        issue_all(dest_next_ref, 1 - slot)

    pltpu.make_async_copy(ys_hbm.at[pl.ds(0, 2 * tt * ROW_TILE), :], buf.at[slot], sem.at[slot]).wait()
    o0 = _tiles_to_rows_load(buf.at[slot, pl.ds(0, tt * ROW_TILE), :], tt)
    o1 = _tiles_to_rows_load(buf.at[slot, pl.ds(tt * ROW_TILE, tt * ROW_TILE), :], tt)
    x1 = _tiles_to_rows_load(x1r_ref, tt)
    gates = jnp.transpose(rinfo_ref[...])
    ffn = gates[:, 0:1] * o0 + gates[:, 1:2] * o1
    o_ref[...] = _layer_norm(alpha * x1 + ffn, g_ref[...], b_ref[...])


def _combine(ys, dest, x1r, rinfo, ln_g, ln_b, n_tok, alpha):
    tt = MOE_TOKEN_TILE
    nb = n_tok // tt
    est = 2 * (tt * D_MODEL * 4 * 2 + tt * V7X_LANES * 4) + 2 * 2 * tt * D_MODEL * 4 + 24 * 1024 * 1024
    return pl.pallas_call(
        functools.partial(_combine_kernel, tt=tt, alpha=alpha),
        grid=(nb,),
        in_specs=[pl.BlockSpec((1, 1, 2 * tt), lambda i: (i, 0, 0), memory_space=pltpu.SMEM),
                  pl.BlockSpec((1, 1, 2 * tt), lambda i: (jnp.minimum(i + 1, nb - 1), 0, 0),
                               memory_space=pltpu.SMEM),
                  pl.BlockSpec(memory_space=pl.ANY),
                  pl.BlockSpec((tt * ROW_TILE, V7X_LANES), lambda i: (i, 0)),
                  pl.BlockSpec((INFO_ROWS, tt), lambda i: (0, i)),
                  pl.BlockSpec((1, D_MODEL), lambda i: (0, 0)),
                  pl.BlockSpec((1, D_MODEL), lambda i: (0, 0))],
        out_specs=pl.BlockSpec((tt, D_MODEL), lambda i: (i, 0)),
        out_shape=jax.ShapeDtypeStruct((n_tok, D_MODEL), f32),
        scratch_shapes=[pltpu.VMEM((2, 2 * tt * ROW_TILE, V7X_LANES), f32),
                        pltpu.SemaphoreType.DMA((2,))],
        compiler_params=_vmem_params(est, ("arbitrary",)),
        name="combine_ln",
    )(dest, dest, ys, x1r, rinfo, ln_g, ln_b)


def _prep_w_in(w_in_l):
    conv = w_in_l[:, :OFF_SBQ]
    o = OFF_SBQ
    sbq = w_in_l[:, o:o + SB_WIDTH] * (SB_HEAD_DIM ** -0.5 * LOG2E)
    sbk = w_in_l[:, o + SB_WIDTH:o + 2 * SB_WIDTH]
    sbv = w_in_l[:, o + 2 * SB_WIDTH:o + 3 * SB_WIDTH]
    o = o + 3 * SB_WIDTH
    gq = w_in_l[:, o:o + GLA_KEY_WIDTH]
    gk = w_in_l[:, o + GLA_KEY_WIDTH:o + 2 * GLA_KEY_WIDTH]
    o = o + 2 * GLA_KEY_WIDTH
    gv = w_in_l[:, o:o + GLA_WIDTH]
    gg = w_in_l[:, o + GLA_WIDTH:o + 2 * GLA_WIDTH]
    lr = w_in_l[:, o + 2 * GLA_WIDTH:]
    zk = jnp.zeros((D_MODEL, GLA_KEY_PAD - GLA_KEY_WIDTH), f32)
    zl = jnp.zeros((D_MODEL, V7X_LANES - GLA_GATE_RANK), f32)
    return jnp.concatenate([conv, sbq, sbk, sbv, gq, zk, gk, zk, gv, gg, lr, zl], axis=1).astype(bf16)


def kernel(x, w_in, conv_w, conv_b, gla_w_gate2, gla_b_gate, gla_norm_w, w_out, ln1_g, ln1_b,
           router_w, router_bias, w1, w3, w2, ln2_g, ln2_b):
    batch, seq, d = x.shape
    depth = w_in.shape[0]
    n_tok = batch * seq
    alpha = (2 * depth) ** 0.25

    w1b = w1.reshape(depth * N_EXPERTS, D_MODEL, D_EXPERT).astype(bf16)
    w3b = w3.reshape(depth * N_EXPERTS, D_MODEL, D_EXPERT).astype(bf16)
    w2b = w2.reshape(depth * N_EXPERTS, D_EXPERT, D_MODEL).astype(bf16)
    rwt = router_w.T.astype(bf16)
    rbt = jnp.broadcast_to(router_bias.reshape(N_EXPERTS, 1), (N_EXPERTS, MIX_TILE))

    x2d = x.reshape(n_tok, d)
    for l in range(depth):
        lw = dict(
            conv_w=jnp.pad(conv_w[l], ((0, V7X_SUBLANES - CONV_K), (0, 0))),
            conv_b=conv_b[l].reshape(1, CONV_WIDTH),
            wg2=jnp.pad(gla_w_gate2[l], ((0, V7X_LANES - GLA_GATE_RANK),
                                         (0, GLA_KEY_PAD - GLA_KEY_WIDTH))).astype(bf16),
            bg=jnp.pad(gla_b_gate[l].reshape(1, GLA_KEY_WIDTH), ((0, 0), (0, GLA_KEY_PAD - GLA_KEY_WIDTH))),
            normw=jnp.tile(gla_norm_w[l], GLA_HEADS).reshape(1, GLA_WIDTH),
            w_out=w_out[l].astype(bf16),
            ln1_g=ln1_g[l].reshape(1, d), ln1_b=ln1_b[l].reshape(1, d),
            rwt=rwt, rbt=rbt)
        proj = _in_proj(x2d, _prep_w_in(w_in[l]))
        ysb = _sb_attention(proj, batch, seq)
        x1r, rinfo, counts = _mix(proj, ysb, x2d, lw, batch, seq, alpha)
        dest, zstart, tile_expert, tile_valid, n_rows = _dest_plan(rinfo, counts, n_tok)
        xs = _dispatch(x1r, dest, zstart, n_tok, n_rows)
        ys = _experts(xs, tile_expert, tile_valid, w1b, w3b, w2b, l, n_rows)
        x2d = _combine(ys, dest, x1r, rinfo, ln2_g[l].reshape(1, d), ln2_b[l].reshape(1, d), n_tok, alpha)
    return x2d.reshape(batch, seq, d)
```

```python
import functools

import jax
import jax.numpy as jnp
from jax import lax
from jax.experimental import pallas as pl
from jax.experimental.pallas import tpu as pltpu

f32 = jnp.float32
bf16 = jnp.bfloat16
i32 = jnp.int32

V7X_LANES = 128
V7X_SUBLANES = 8
V7X_VMEM_BYTES = 64 * 1024 * 1024

D_MODEL = 1024
CONV_WIDTH = 256
CONV_K = 3
SB_HEADS = 6
SB_HEAD_DIM = 64
SB_WIDTH = SB_HEADS * SB_HEAD_DIM
SB_BLOCK = 128
SB_QT = 512
SB_KT = 2 * SB_BLOCK
GLA_HEADS = 4
GLA_DV = 96
GLA_DK = 48
GLA_WIDTH = GLA_HEADS * GLA_DV
GLA_KEY_WIDTH = GLA_HEADS * GLA_DK
GLA_KEY_PAD = 256
GLA_GATE_RANK = 16
GLA_GATE_NORM = 16.0
GLA_CHUNK = 64
N_EXPERTS = 16
N_GROUPS = 4
EXPERTS_PER_GROUP = 4
D_EXPERT = 512
LN_EPS = 1e-5
RMS_EPS = 1e-6
LOG2E = 1.4426950408889634

SB_PROJ_WIDTH = 3 * SB_WIDTH
OFF_CONV = 0
OFF_GQ = 3 * CONV_WIDTH
OFF_GK = OFF_GQ + GLA_KEY_PAD
OFF_GV = OFF_GK + GLA_KEY_PAD
OFF_GG = OFF_GV + GLA_WIDTH
OFF_LR = OFF_GG + GLA_WIDTH
R_WIDTH = OFF_LR + V7X_LANES
P_WIDTH = SB_PROJ_WIDTH + R_WIDTH

ROW_TILE = V7X_SUBLANES
MIX_TILE = 1024
MOE_TOKEN_TILE = 512
MOE_ROW_TILE = 512
DMA_UNROLL = 8
INFO_ROWS = V7X_SUBLANES


def _vmem_params(est_bytes, semantics):
    limit = int(min(V7X_VMEM_BYTES - 6 * 1024 * 1024, max(est_bytes, 16 * 1024 * 1024)))
    return pltpu.CompilerParams(dimension_semantics=semantics, vmem_limit_bytes=limit)


def _neg_abs(x):
    bits = lax.bitcast_convert_type(x, jnp.uint32) | jnp.uint32(0x80000000)
    return lax.bitcast_convert_type(bits, f32)


def _rows_to_tiles_store(ref, val, n_rows):
    for j in range(D_MODEL // V7X_LANES):
        ref[pl.ds(j, n_rows, stride=ROW_TILE), :] = val[:, j * V7X_LANES:(j + 1) * V7X_LANES]


def _tiles_to_rows_load(ref, n_rows):
    return jnp.concatenate(
        [ref[pl.ds(j, n_rows, stride=ROW_TILE), :] for j in range(D_MODEL // V7X_LANES)], axis=1)


def _layer_norm(h, g, b):
    mu = jnp.mean(h, axis=-1, keepdims=True)
    c = h - mu
    var = jnp.mean(c * c, axis=-1, keepdims=True)
    return c * lax.rsqrt(var + LN_EPS) * g + b


def _in_proj_kernel(x_ref, w_ref, osb_ref, orest_ref):
    xb = x_ref[...].astype(bf16)
    n_chunk = 256
    for o_ref, w0, width in ((osb_ref, 0, SB_PROJ_WIDTH), (orest_ref, SB_PROJ_WIDTH, R_WIDTH)):
        for n0 in range(0, width, n_chunk):
            n1 = min(n0 + n_chunk, width)
            o_ref[:, n0:n1] = jnp.dot(
                xb, w_ref[:, w0 + n0:w0 + n1], preferred_element_type=f32).astype(bf16)


def _in_proj(x2d, w):
    t = x2d.shape[0]
    tm = 512
    est = 2 * (tm * D_MODEL * 4 + D_MODEL * P_WIDTH * 2 + tm * P_WIDTH * 2) + 8 * 1024 * 1024
    return pl.pallas_call(
        _in_proj_kernel,
        grid=(t // tm,),
        in_specs=[pl.BlockSpec((tm, D_MODEL), lambda i: (i, 0)),
                  pl.BlockSpec((D_MODEL, P_WIDTH), lambda i: (0, 0))],
        out_specs=[pl.BlockSpec((tm, SB_PROJ_WIDTH), lambda i: (i, 0)),
                   pl.BlockSpec((tm, R_WIDTH), lambda i: (i, 0))],
        out_shape=[jax.ShapeDtypeStruct((t, SB_PROJ_WIDTH), bf16),
                   jax.ShapeDtypeStruct((t, R_WIDTH), bf16)],
        compiler_params=_vmem_params(est, ("arbitrary",)),
        name="in_proj",
    )(x2d, w)


def _sb_kernel(q_ref, k_ref, v_ref, o_ref, acc_ref, car_ref, *, seq):
    blk = SB_BLOCK
    qt = SB_QT
    n_pairs = SB_WIDTH // V7X_LANES
    klane = lax.broadcasted_iota(i32, (blk, V7X_LANES), 1)
    head_mask = [klane < SB_HEAD_DIM, klane >= SB_HEAD_DIM]
    wr = lax.broadcasted_iota(i32, (2 * blk, 2 * blk), 0)
    wc = lax.broadcasted_iota(i32, (2 * blk, 2 * blk), 1)
    suffix_w = jnp.where(((wr >= blk) == (wc >= blk)) & (wr >= wc), 1.0, 0.0).astype(bf16)
    def softplus2(z):
        return jnp.maximum(z, 0.0) + jnp.log2(1.0 + jnp.exp2(_neg_abs(z)))

    def lane_bcast(tot):
        n = tot.shape[0]
        return jnp.concatenate([jnp.broadcast_to(tot[:, 0:1], (n, blk)),
                                jnp.broadcast_to(tot[:, blk:blk + 1], (n, blk))], axis=1)

    def span(q_tiles, k0, diag_j):
        masked = diag_j is not None
        first = masked and diag_j == qt // SB_KT - 1
        r0 = diag_j * SB_KT if masked else 0
        nr = qt - r0
        if masked:
            ri = lax.broadcasted_iota(i32, (nr, 4 * blk), 0)
            ci = lax.broadcasted_iota(i32, (nr, 4 * blk), 1)
            diag_valid = (ci & (blk - 1)) + jnp.where(ci >= 2 * blk, blk, 0) < ri
        k_a = k_ref[pl.ds(k0, blk), :]
        k_b = k_ref[pl.ds(k0 + blk, blk), :]
        v_a = v_ref[pl.ds(k0, blk), :]
        v_b = v_ref[pl.ds(k0 + blk, blk), :]
        for p in range(n_pairs):
            sl = slice(p * V7X_LANES, (p + 1) * V7X_LANES)

            def stack(x_a, x_b):
                zero = jnp.zeros((blk, V7X_LANES), bf16)
                return jnp.concatenate([jnp.where(head_mask[0], x_a[:, sl], zero),
                                        jnp.where(head_mask[1], x_a[:, sl], zero),
                                        jnp.where(head_mask[0], x_b[:, sl], zero),
                                        jnp.where(head_mask[1], x_b[:, sl], zero)], axis=0)

            z = lax.dot_general(q_tiles[p][r0:], stack(k_a, k_b), (((1,), (1,)), ((), ())),
                                preferred_element_type=f32)
            sp = softplus2(z)
            if masked:
                sp = jnp.where(diag_valid, sp, 0.0)
            spb = sp.astype(bf16)
            tot_b = jnp.dot(spb[:, 2 * blk:], suffix_w, preferred_element_type=f32)
            if not first:
                tot_b = tot_b + car_ref[p, r0:, :]
            tot_a = jnp.dot(spb[:, :2 * blk], suffix_w, preferred_element_type=f32) + lane_bcast(tot_b)
            car_ref[p, r0:, :] = lane_bcast(tot_a)
            a = jnp.exp2(z - jnp.concatenate([tot_a, tot_b], axis=1))
            if masked:
                a = jnp.where(diag_valid, a, 0.0)
            av = jnp.dot(a.astype(bf16), stack(v_a, v_b), preferred_element_type=f32)
            if first:
                acc_ref[p, r0:, :] = av
                if r0 > 0:
                    acc_ref[p, :r0, :] = jnp.zeros((r0, V7X_LANES), f32)
                    car_ref[p, :r0, :] = jnp.zeros((r0, 2 * blk), f32)
            else:
                acc_ref[p, r0:, :] += av

    def q_tile(qi, carry):
        q0 = pl.multiple_of(qi * qt, qt)
        q_all = q_ref[pl.ds(q0, qt), :]
        q_tiles = [q_all[:, p * V7X_LANES:(p + 1) * V7X_LANES] for p in range(n_pairs)]
        for j in reversed(range(qt // SB_KT)):
            span(q_tiles, q0 + j * SB_KT, j)

        def k_span(i, c):
            span(q_tiles, pl.multiple_of(q0 - (i + 1) * SB_KT, SB_KT), None)
            return c

        lax.fori_loop(0, qi * (qt // SB_KT), k_span, 0)
        o_ref[pl.ds(q0, qt), :] = jnp.concatenate(
            [acc_ref[p] for p in range(n_pairs)], axis=1).astype(o_ref.dtype)
        return carry

    lax.fori_loop(0, seq // qt, q_tile, 0)


def _sb_attention(proj_sb, batch, seq):
    t = batch * seq
    n_pairs = SB_WIDTH // V7X_LANES
    est = 2 * 4 * seq * SB_WIDTH * 2 + n_pairs * SB_QT * 3 * V7X_LANES * 4 + 24 * 1024 * 1024
    return pl.pallas_call(
        functools.partial(_sb_kernel, seq=seq),
        grid=(batch,),
        in_specs=[pl.BlockSpec((seq, SB_WIDTH), lambda b: (b, 0)),
                  pl.BlockSpec((seq, SB_WIDTH), lambda b: (b, 1)),
                  pl.BlockSpec((seq, SB_WIDTH), lambda b: (b, 2))],
        out_specs=pl.BlockSpec((seq, SB_WIDTH), lambda b: (b, 0)),
        out_shape=jax.ShapeDtypeStruct((t, SB_WIDTH), bf16),
        scratch_shapes=[pltpu.VMEM((n_pairs, SB_QT, V7X_LANES), f32),
                        pltpu.VMEM((n_pairs, SB_QT, 2 * SB_BLOCK), f32)],
        compiler_params=_vmem_params(est, ("arbitrary",)),
        name="sb_attention",
    )(proj_sb, proj_sb, proj_sb)


def _route(sel, scores):
    c, s = sel, scores
    group_score = []
    for g in range(N_GROUPS):
        a, b, cc, d = c[4 * g:4 * g + 4]
        hi1, lo1 = jnp.maximum(a, b), jnp.minimum(a, b)
        hi2, lo2 = jnp.maximum(cc, d), jnp.minimum(cc, d)
        top1 = jnp.maximum(hi1, hi2)
        top2 = jnp.maximum(jnp.minimum(hi1, hi2), jnp.maximum(lo1, lo2))
        group_score.append(top1 + top2)
    best = group_score[0]
    gidx = jnp.zeros_like(best, dtype=i32)
    for g in range(1, N_GROUPS):
        better = group_score[g] > best
        gidx = jnp.where(better, g, gidx)
        best = jnp.where(better, group_score[g], best)
    vals, scs = [], []
    for j in range(EXPERTS_PER_GROUP):
        vj, sj = c[j], s[j]
        for g in range(1, N_GROUPS):
            pick = gidx == g
            vj = jnp.where(pick, c[4 * g + j], vj)
            sj = jnp.where(pick, s[4 * g + j], sj)
        vals.append(vj)
        scs.append(sj)
    m1, i1, w1 = vals[0], jnp.zeros_like(gidx), scs[0]
    for j in range(1, EXPERTS_PER_GROUP):
        better = vals[j] > m1
        m1 = jnp.where(better, vals[j], m1)
        i1 = jnp.where(better, j, i1)
        w1 = jnp.where(better, scs[j], w1)
    m2 = jnp.full_like(m1, -jnp.inf)
    i2 = jnp.full_like(i1, -1)
    w2 = jnp.zeros_like(w1)
    for j in range(EXPERTS_PER_GROUP):
        better = (i1 != j) & ((vals[j] > m2) | (i2 < 0))
        m2 = jnp.where(better, vals[j], m2)
        i2 = jnp.where(better, j, i2)
        w2 = jnp.where(better, scs[j], w2)
    denom = w1 + w2
    return w1 / denom, w2 / denom, gidx * EXPERTS_PER_GROUP + i1, gidx * EXPERTS_PER_GROUP + i2


def _mix_kernel(proj_ref, ysb_ref, x_ref, convw_ref, convb_ref, wg2_ref, bg_ref, normw_ref,
                wout_ref, lng_ref, lnb_ref, rwt_ref, rbt_ref,
                x1r_ref, rinfo_ref, counts_ref,
                halo_ref, state_ref, cnt_ref, *, ts, alpha):
    t_idx = pl.program_id(1)
    kp = GLA_KEY_PAD
    ck = GLA_CHUNK

    @pl.when((pl.program_id(0) == 0) & (t_idx == 0))
    def _():
        cnt_ref[...] = jnp.zeros_like(cnt_ref)

    @pl.when(t_idx == 0)
    def _():
        halo_ref[...] = jnp.zeros_like(halo_ref)
        state_ref[...] = jnp.zeros_like(state_ref)

    c_h = proj_ref[:, OFF_CONV:OFF_CONV + CONV_WIDTH].astype(f32)
    c_b = proj_ref[:, OFF_CONV + CONV_WIDTH:OFF_CONV + 2 * CONV_WIDTH].astype(f32)
    c_c = proj_ref[:, OFF_CONV + 2 * CONV_WIDTH:OFF_CONV + 3 * CONV_WIDTH].astype(f32)
    u = c_c * c_h
    ucat = jnp.concatenate([halo_ref[...], u], axis=0)
    halo_ref[...] = u[ts - V7X_SUBLANES:, :]
    u1 = ucat[V7X_SUBLANES - 1:V7X_SUBLANES - 1 + ts, :]
    u2 = ucat[V7X_SUBLANES - 2:V7X_SUBLANES - 2 + ts, :]
    y = convb_ref[...] + convw_ref[0:1, :] * u
    y = y + convw_ref[1:2, :] * u1
    y = y + convw_ref[2:3, :] * u2
    y_conv = c_b * y

    lr = proj_ref[:, OFF_LR:OFF_LR + V7X_LANES]
    pre = jnp.dot(lr, wg2_ref[...], preferred_element_type=f32) + bg_ref[...]
    log_a = (jnp.minimum(pre, 0.0) - jnp.log1p(jnp.exp(_neg_abs(pre)))) * (1.0 / GLA_GATE_NORM)
    chunk_shift = ck.bit_length() - 1
    dck = 2 * ck
    ir = lax.broadcasted_iota(i32, (dck, dck), 0)
    ic = lax.broadcasted_iota(i32, (dck, dck), 1)
    same_chunk = (ir >> chunk_shift) == (ic >> chunk_shift)
    intra_mask = same_chunk & (ic <= ir)
    cum_w = jnp.where(intra_mask, 1.0, 0.0).astype(bf16)
    tot_w = jnp.where(same_chunk, 1.0, 0.0).astype(bf16)
    cum_tot_w = jnp.concatenate([jnp.concatenate([cum_w, cum_w], axis=1),
                                 jnp.concatenate([tot_w, tot_w], axis=1)], axis=0)
    la_hi = log_a.astype(bf16)
    la_lo = (log_a - la_hi.astype(f32)).astype(bf16)
    bcum, blast = [], []
    for j in range(ts // dck):
        rows = slice(j * dck, (j + 1) * dck)
        r = jnp.dot(cum_tot_w, jnp.concatenate([la_hi[rows], la_lo[rows]], axis=0),
                    preferred_element_type=f32)
        bcum.append(r[:dck])
        blast.append(r[dck:])
    bcum = jnp.concatenate(bcum, axis=0)
    blast = jnp.concatenate(blast, axis=0)
    qf = proj_ref[:, OFF_GQ:OFF_GQ + kp].astype(f32)
    kf = proj_ref[:, OFF_GK:OFF_GK + kp].astype(f32)
    vb = proj_ref[:, OFF_GV:OFF_GV + GLA_WIDTH]
    q_in = ((qf * (GLA_DK ** -0.5)) * jnp.exp(bcum)).astype(bf16)
    k_in = (kf * jnp.exp(-bcum)).astype(bf16)
    k_te = (kf * jnp.exp(blast - bcum)).astype(bf16)
    decay = jnp.exp(blast)

    klane = lax.broadcasted_iota(i32, (dck, kp), 1)
    vlane = lax.broadcasted_iota(i32, (dck, GLA_WIDTH), 1)
    kmask = [(klane >= h * GLA_DK) & (klane < (h + 1) * GLA_DK) for h in range(GLA_HEADS)]
    vmask = [(vlane >= h * GLA_DV) & (vlane < (h + 1) * GLA_DV) for h in range(GLA_HEADS)]
    o_intra = []
    for j in range(ts // dck):
        rows = slice(j * dck, (j + 1) * dck)
        q_j, k_j, v_j = q_in[rows], k_in[rows], vb[rows]
        o_j = jnp.zeros((dck, GLA_WIDTH), f32)
        for h in range(GLA_HEADS):
            qh = jnp.where(kmask[h], q_j, jnp.zeros_like(q_j))
            sc = lax.dot_general(qh, k_j, (((1,), (1,)), ((), ())), preferred_element_type=f32)
            sc = jnp.where(intra_mask, sc, 0.0).astype(bf16)
            o_j = jnp.where(vmask[h], jnp.dot(sc, v_j, preferred_element_type=f32), o_j)
        o_intra.append(o_j)
    o_intra = jnp.concatenate(o_intra, axis=0)

    srow = lax.broadcasted_iota(i32, (GLA_WIDTH, kp), 0)
    scol = lax.broadcasted_iota(i32, (GLA_WIDTH, kp), 1)
    blockdiag = jnp.zeros((GLA_WIDTH, kp), jnp.bool_)
    for h in range(GLA_HEADS):
        blockdiag = blockdiag | ((srow >= h * GLA_DV) & (srow < (h + 1) * GLA_DV)
                                 & (scol >= h * GLA_DK) & (scol < (h + 1) * GLA_DK))
    kvt = [lax.dot_general(vb[c * ck:(c + 1) * ck], k_te[c * ck:(c + 1) * ck], (((0,), (0,)), ((), ())),
                           preferred_element_type=f32) for c in range(ts // ck)]
    state = state_ref[...]
    o_inter = []
    for c in range(ts // ck):
        o_inter.append(lax.dot_general(q_in[c * ck:(c + 1) * ck], state.astype(bf16),
                                       (((1,), (1,)), ((), ())), preferred_element_type=f32))
        state = decay[c * ck:c * ck + 1, :] * state + jnp.where(blockdiag, kvt[c], 0.0)
    state_ref[...] = state
    o_all = o_intra + jnp.concatenate(o_inter, axis=0)

    o_sq = o_all * o_all
    vlane_t = lax.broadcasted_iota(i32, (ts, GLA_WIDTH), 1)
    inv = jnp.zeros((ts, GLA_WIDTH), f32)
    for h in range(GLA_HEADS):
        hm = (vlane_t >= h * GLA_DV) & (vlane_t < (h + 1) * GLA_DV)
        ms = jnp.sum(jnp.where(hm, o_sq, 0.0), axis=-1, keepdims=True) * (1.0 / GLA_DV)
        inv = jnp.where(hm, lax.rsqrt(ms + RMS_EPS), inv)
    gg = proj_ref[:, OFF_GG:OFF_GG + GLA_WIDTH].astype(f32)
    y_gla = (o_all * inv * normw_ref[...]) * (gg * (1.0 / (1.0 + jnp.exp(-gg))))

    mix = jnp.concatenate([y_conv.astype(bf16), ysb_ref[...], y_gla.astype(bf16)], axis=1)
    proj_out = jnp.dot(mix, wout_ref[...], preferred_element_type=f32)
    x1 = _layer_norm(alpha * x_ref[...] + proj_out, lng_ref[...], lnb_ref[...])
    _rows_to_tiles_store(x1r_ref, x1, ts)

    logits = lax.dot_general(rwt_ref[...], x1.astype(bf16), (((1,), (1,)), ((), ())),
                             preferred_element_type=f32)
    scores = 1.0 / (1.0 + jnp.exp(-logits))
    sel = scores + rbt_ref[...]
    g0, g1, e0, e1 = _route([sel[e:e + 1, :] for e in range(N_EXPERTS)],
                            [scores[e:e + 1, :] for e in range(N_EXPERTS)])
    eid = lax.broadcasted_iota(i32, (N_EXPERTS, ts), 0)
    pick0 = eid == e0
    pick1 = eid == e1
    onehot = jnp.where(pick0 | pick1, 1.0, 0.0)
    rr = lax.broadcasted_iota(i32, (ts, ts), 0)
    cc = lax.broadcasted_iota(i32, (ts, ts), 1)
    before = jnp.dot(onehot.astype(bf16), jnp.where(rr < cc, 1.0, 0.0).astype(bf16),
                     preferred_element_type=f32) + cnt_ref[...]
    rank0 = jnp.sum(jnp.where(pick0, before, 0.0), axis=0, keepdims=True)
    rank1 = jnp.sum(jnp.where(pick1, before, 0.0), axis=0, keepdims=True)
    new_cnt = cnt_ref[...] + jnp.sum(onehot, axis=1, keepdims=True)
    cnt_ref[...] = new_cnt
    counts_ref[...] = new_cnt[:, :V7X_LANES]
    zero = jnp.zeros_like(g0)
    rinfo_ref[...] = jnp.concatenate(
        [g0, g1, e0.astype(f32), e1.astype(f32), rank0, rank1, zero, zero], axis=0)


def _mix(proj, ysb, x2d, lw, batch, seq, alpha):
    t = batch * seq
    ts = MIX_TILE
    nt = seq // ts
    row = lambda b, s: (b * nt + s, 0)
    const = lambda b, s: (0, 0)
    est = (2 * (ts * R_WIDTH * 2 + ts * SB_WIDTH * 2 + 2 * ts * D_MODEL * 4 + D_MODEL * D_MODEL * 2)
           + 28 * 1024 * 1024)
    return pl.pallas_call(
        functools.partial(_mix_kernel, ts=ts, alpha=alpha),
        grid=(batch, nt),
        in_specs=[pl.BlockSpec((ts, R_WIDTH), row),
                  pl.BlockSpec((ts, SB_WIDTH), row),
                  pl.BlockSpec((ts, D_MODEL), row),
                  pl.BlockSpec((V7X_SUBLANES, CONV_WIDTH), const),
                  pl.BlockSpec((1, CONV_WIDTH), const),
                  pl.BlockSpec((V7X_LANES, GLA_KEY_PAD), const),
                  pl.BlockSpec((1, GLA_KEY_PAD), const),
                  pl.BlockSpec((1, GLA_WIDTH), const),
                  pl.BlockSpec((D_MODEL, D_MODEL), const),
                  pl.BlockSpec((1, D_MODEL), const),
                  pl.BlockSpec((1, D_MODEL), const),
                  pl.BlockSpec((N_EXPERTS, D_MODEL), const),
                  pl.BlockSpec((N_EXPERTS, ts), const)],
        out_specs=[pl.BlockSpec((ts * ROW_TILE, V7X_LANES), row),
                   pl.BlockSpec((INFO_ROWS, ts), lambda b, s: (0, b * nt + s)),
                   pl.BlockSpec((N_EXPERTS, V7X_LANES), const)],
        out_shape=[jax.ShapeDtypeStruct((t * ROW_TILE, V7X_LANES), f32),
                   jax.ShapeDtypeStruct((INFO_ROWS, t), f32),
                   jax.ShapeDtypeStruct((N_EXPERTS, V7X_LANES), f32)],
        scratch_shapes=[pltpu.VMEM((V7X_SUBLANES, CONV_WIDTH), f32),
                        pltpu.VMEM((GLA_WIDTH, GLA_KEY_PAD), f32),
                        pltpu.VMEM((N_EXPERTS, ts), f32)],
        compiler_params=_vmem_params(est, ("arbitrary", "arbitrary")),
        name="mix_ln_router",
    )(proj, ysb, x2d, lw["conv_w"], lw["conv_b"], lw["wg2"], lw["bg"], lw["normw"], lw["w_out"],
      lw["ln1_g"], lw["ln1_b"], lw["rwt"], lw["rbt"])


def _dest_plan(rinfo, counts, n_tok):
    tm = MOE_ROW_TILE
    n_rows = 2 * n_tok + N_EXPERTS * tm
    cnt = counts[:, 0].astype(i32)
    padded = ((cnt + tm - 1) // tm) * tm
    ends = jnp.cumsum(padded)
    starts = ends - padded
    ids = jnp.arange(N_EXPERTS, dtype=i32)[None, :]
    e0 = rinfo[2].astype(i32)
    e1 = rinfo[3].astype(i32)
    dest0 = rinfo[4].astype(i32) + jnp.sum(jnp.where(e0[:, None] == ids, starts[None, :], 0), axis=1)
    dest1 = rinfo[5].astype(i32) + jnp.sum(jnp.where(e1[:, None] == ids, starts[None, :], 0), axis=1)
    nb = n_tok // MOE_TOKEN_TILE
    dest = jnp.concatenate([dest0.reshape(nb, 1, MOE_TOKEN_TILE), dest1.reshape(nb, 1, MOE_TOKEN_TILE)], axis=2)
    tile_start = jnp.arange(n_rows // tm, dtype=i32) * tm
    tile_expert = jnp.minimum(jnp.sum((tile_start[:, None] >= ends[None, :]).astype(i32), axis=1),
                              N_EXPERTS - 1).astype(i32)
    tile_valid = (tile_start < ends[-1]).astype(i32)
    tail = ends[-1] + jnp.arange(N_EXPERTS, dtype=i32) * tm
    zstart = jnp.concatenate([jnp.where(padded > 0, ends - tm, -1),
                              jnp.where(tail < n_rows, tail, -1)]).astype(i32)
    return dest, zstart, tile_expert, tile_valid, n_rows


def _dispatch_kernel(zstart_ref, dest_ref, x1r_ref, xs_hbm, zbuf, sem, zsem, *, tt):
    @pl.when(pl.program_id(0) == 0)
    def _():
        zbuf[...] = jnp.zeros_like(zbuf)

        def zero_copy(e):
            z0 = pl.multiple_of(jnp.maximum(zstart_ref[e], 0) * ROW_TILE, ROW_TILE)
            return pltpu.make_async_copy(zbuf, xs_hbm.at[pl.ds(z0, MOE_ROW_TILE * ROW_TILE), :], zsem)

        for e in range(2 * N_EXPERTS):
            @pl.when(zstart_ref[e] >= 0)
            def _():
                zero_copy(e).start()
        for e in range(2 * N_EXPERTS):
            @pl.when(zstart_ref[e] >= 0)
            def _():
                zero_copy(e).wait()

    def row_copy(r, j):
        d = dest_ref[0, 0, j]
        return pltpu.make_async_copy(
            x1r_ref.at[pl.ds(pl.multiple_of(r * ROW_TILE, ROW_TILE), ROW_TILE), :],
            xs_hbm.at[pl.ds(pl.multiple_of(d * ROW_TILE, ROW_TILE), ROW_TILE), :], sem)

    def issue(g, c):
        for u in range(DMA_UNROLL):
            r = g * DMA_UNROLL + u
            row_copy(r, r).start(priority=0)
            row_copy(r, tt + r).start(priority=1)
        return c

    lax.fori_loop(0, tt // DMA_UNROLL, issue, 0)
    whole = pltpu.make_async_copy(x1r_ref, xs_hbm.at[pl.ds(0, tt * ROW_TILE), :], sem)
    whole.wait()
    whole.wait()


def _dispatch(x1r, dest, zstart, n_tok, n_rows):
    tt = MOE_TOKEN_TILE
    est = 2 * tt * D_MODEL * 4 + MOE_ROW_TILE * D_MODEL * 4 + 8 * 1024 * 1024
    grid_spec = pltpu.PrefetchScalarGridSpec(
        num_scalar_prefetch=1,
        grid=(n_tok // tt,),
        in_specs=[pl.BlockSpec((1, 1, 2 * tt), lambda i, z: (i, 0, 0), memory_space=pltpu.SMEM),
                  pl.BlockSpec((tt * ROW_TILE, V7X_LANES), lambda i, z: (i, 0))],
        out_specs=pl.BlockSpec(memory_space=pl.ANY),
        scratch_shapes=[pltpu.VMEM((MOE_ROW_TILE * ROW_TILE, V7X_LANES), f32),
                        pltpu.SemaphoreType.DMA(()),
                        pltpu.SemaphoreType.DMA(())],
    )
    return pl.pallas_call(
        functools.partial(_dispatch_kernel, tt=tt),
        grid_spec=grid_spec,
        out_shape=jax.ShapeDtypeStruct((n_rows * ROW_TILE, V7X_LANES), f32),
        compiler_params=_vmem_params(est, ("arbitrary",)),
        name="dispatch",
    )(zstart, dest, x1r)


def _expert_kernel(texp_ref, tvalid_ref, xs_ref, w1_ref, w3_ref, w2_ref, ys_ref, *, tm):
    @pl.when(tvalid_ref[pl.program_id(0)] > 0)
    def _():
        xb = _tiles_to_rows_load(xs_ref, tm).astype(bf16)
        a = jnp.dot(xb, w1_ref[0], preferred_element_type=f32)
        b = jnp.dot(xb, w3_ref[0], preferred_element_type=f32)
        h = (a * (1.0 / (1.0 + jnp.exp(-a)))) * b
        o = jnp.dot(h.astype(bf16), w2_ref[0], preferred_element_type=f32)
        _rows_to_tiles_store(ys_ref, o, tm)

    @pl.when(tvalid_ref[pl.program_id(0)] == 0)
    def _():
        ys_ref[...] = jnp.zeros_like(ys_ref)


def _experts(xs, tile_expert, tile_valid, w1b, w3b, w2b, layer, n_rows):
    tm = MOE_ROW_TILE
    base = layer * N_EXPERTS
    blk = (tm * ROW_TILE, V7X_LANES)
    est = 2 * (3 * D_MODEL * D_EXPERT * 2 + 2 * tm * D_MODEL * 4) + 16 * 1024 * 1024
    grid_spec = pltpu.PrefetchScalarGridSpec(
        num_scalar_prefetch=2,
        grid=(n_rows // tm,),
        in_specs=[pl.BlockSpec(blk, lambda i, te, tv: (i, 0)),
                  pl.BlockSpec((1, D_MODEL, D_EXPERT), lambda i, te, tv: (base + te[i], 0, 0)),
                  pl.BlockSpec((1, D_MODEL, D_EXPERT), lambda i, te, tv: (base + te[i], 0, 0)),
                  pl.BlockSpec((1, D_EXPERT, D_MODEL), lambda i, te, tv: (base + te[i], 0, 0))],
        out_specs=pl.BlockSpec(blk, lambda i, te, tv: (i, 0)),
    )
    return pl.pallas_call(
        functools.partial(_expert_kernel, tm=tm),
        grid_spec=grid_spec,
        out_shape=jax.ShapeDtypeStruct((n_rows * ROW_TILE, V7X_LANES), f32),
        compiler_params=_vmem_params(est, ("arbitrary",)),
        name="experts",
    )(tile_expert, tile_valid, xs, w1b, w3b, w2b)


def _combine_kernel(dest_ref, dest_next_ref, ys_hbm, x1r_ref, rinfo_ref, g_ref, b_ref, o_ref,
                    buf, sem, *, tt, alpha):
    i = pl.program_id(0)
    n = pl.num_programs(0)
    slot = lax.rem(i, 2)

    def issue_all(d_ref, s):
        def issue(g, c):
            for u in range(DMA_UNROLL):
                j = g * DMA_UNROLL + u
                d = d_ref[0, 0, j]
                pltpu.make_async_copy(
                    ys_hbm.at[pl.ds(pl.multiple_of(d * ROW_TILE, ROW_TILE), ROW_TILE), :],
                    buf.at[s, pl.ds(pl.multiple_of(j * ROW_TILE, ROW_TILE), ROW_TILE), :],
                    sem.at[s]).start(priority=u % 2)
            return c
        lax.fori_loop(0, 2 * tt // DMA_UNROLL, issue, 0)

    @pl.when(i == 0)
    def _():
        issue_all(dest_ref, 0)

    @pl.when(i + 1 < n)
    def _():
        issue_all(dest_next_ref, 1 - slot)

    pltpu.make_async_copy(ys_hbm.at[pl.ds(0, 2 * tt * ROW_TILE), :], buf.at[slot], sem.at[slot]).wait()
    o0 = _tiles_to_rows_load(buf.at[slot, pl.ds(0, tt * ROW_TILE), :], tt)
    o1 = _tiles_to_rows_load(buf.at[slot, pl.ds(tt * ROW_TILE, tt * ROW_TILE), :], tt)
    x1 = _tiles_to_rows_load(x1r_ref, tt)
    gates = jnp.transpose(rinfo_ref[...])
    ffn = gates[:, 0:1] * o0 + gates[:, 1:2] * o1
    o_ref[...] = _layer_norm(alpha * x1 + ffn, g_ref[...], b_ref[...])


def _combine(ys, dest, x1r, rinfo, ln_g, ln_b, n_tok, alpha):
    tt = MOE_TOKEN_TILE
    nb = n_tok // tt
    est = 2 * (tt * D_MODEL * 4 * 2 + tt * V7X_LANES * 4) + 2 * 2 * tt * D_MODEL * 4 + 24 * 1024 * 1024
    return pl.pallas_call(
        functools.partial(_combine_kernel, tt=tt, alpha=alpha),
        grid=(nb,),
        in_specs=[pl.BlockSpec((1, 1, 2 * tt), lambda i: (i, 0, 0), memory_space=pltpu.SMEM),
                  pl.BlockSpec((1, 1, 2 * tt), lambda i: (jnp.minimum(i + 1, nb - 1), 0, 0),
                               memory_space=pltpu.SMEM),
                  pl.BlockSpec(memory_space=pl.ANY),
                  pl.BlockSpec((tt * ROW_TILE, V7X_LANES), lambda i: (i, 0)),
                  pl.BlockSpec((INFO_ROWS, tt), lambda i: (0, i)),
                  pl.BlockSpec((1, D_MODEL), lambda i: (0, 0)),
                  pl.BlockSpec((1, D_MODEL), lambda i: (0, 0))],
        out_specs=pl.BlockSpec((tt, D_MODEL), lambda i: (i, 0)),
        out_shape=jax.ShapeDtypeStruct((n_tok, D_MODEL), f32),
        scratch_shapes=[pltpu.VMEM((2, 2 * tt * ROW_TILE, V7X_LANES), f32),
                        pltpu.SemaphoreType.DMA((2,))],
        compiler_params=_vmem_params(est, ("arbitrary",)),
        name="combine_ln",
    )(dest, dest, ys, x1r, rinfo, ln_g, ln_b)


def _prep_w_in(w_in_l):
    o = 3 * CONV_WIDTH
    conv = w_in_l[:, :o]
    sbq = w_in_l[:, o:o + SB_WIDTH] * (SB_HEAD_DIM ** -0.5 * LOG2E)
    sbk = w_in_l[:, o + SB_WIDTH:o + 2 * SB_WIDTH]
    sbv = w_in_l[:, o + 2 * SB_WIDTH:o + 3 * SB_WIDTH]
    o = o + 3 * SB_WIDTH
    gq = w_in_l[:, o:o + GLA_KEY_WIDTH]
    gk = w_in_l[:, o + GLA_KEY_WIDTH:o + 2 * GLA_KEY_WIDTH]
    o = o + 2 * GLA_KEY_WIDTH
    gv = w_in_l[:, o:o + GLA_WIDTH]
    gg = w_in_l[:, o + GLA_WIDTH:o + 2 * GLA_WIDTH]
    lr = w_in_l[:, o + 2 * GLA_WIDTH:]
    zk = jnp.zeros((D_MODEL, GLA_KEY_PAD - GLA_KEY_WIDTH), f32)
    zl = jnp.zeros((D_MODEL, V7X_LANES - GLA_GATE_RANK), f32)
    return jnp.concatenate([sbq, sbk, sbv, conv, gq, zk, gk, zk, gv, gg, lr, zl], axis=1).astype(bf16)


def kernel(x, w_in, conv_w, conv_b, gla_w_gate2, gla_b_gate, gla_norm_w, w_out, ln1_g, ln1_b,
           router_w, router_bias, w1, w3, w2, ln2_g, ln2_b):
    batch, seq, d = x.shape
    depth = w_in.shape[0]
    n_tok = batch * seq
    alpha = (2 * depth) ** 0.25

    w1b = w1.reshape(depth * N_EXPERTS, D_MODEL, D_EXPERT).astype(bf16)
    w3b = w3.reshape(depth * N_EXPERTS, D_MODEL, D_EXPERT).astype(bf16)
    w2b = w2.reshape(depth * N_EXPERTS, D_EXPERT, D_MODEL).astype(bf16)
    rwt = router_w.T.astype(bf16)
    rbt = jnp.broadcast_to(router_bias.reshape(N_EXPERTS, 1), (N_EXPERTS, MIX_TILE))

    x2d = x.reshape(n_tok, d)
    for l in range(depth):
        lw = dict(
            conv_w=jnp.pad(conv_w[l], ((0, V7X_SUBLANES - CONV_K), (0, 0))),
            conv_b=conv_b[l].reshape(1, CONV_WIDTH),
            wg2=jnp.pad(gla_w_gate2[l], ((0, V7X_LANES - GLA_GATE_RANK),
                                         (0, GLA_KEY_PAD - GLA_KEY_WIDTH))).astype(bf16),
            bg=jnp.pad(gla_b_gate[l].reshape(1, GLA_KEY_WIDTH), ((0, 0), (0, GLA_KEY_PAD - GLA_KEY_WIDTH))),
            normw=jnp.tile(gla_norm_w[l], GLA_HEADS).reshape(1, GLA_WIDTH),
            w_out=w_out[l].astype(bf16),
            ln1_g=ln1_g[l].reshape(1, d), ln1_b=ln1_b[l].reshape(1, d),
            rwt=rwt, rbt=rbt)
        proj_sb, proj_rest = _in_proj(x2d, _prep_w_in(w_in[l]))
        ysb = _sb_attention(proj_sb, batch, seq)
        x1r, rinfo, counts = _mix(proj_rest, ysb, x2d, lw, batch, seq, alpha)
        dest, zstart, tile_expert, tile_valid, n_rows = _dest_plan(rinfo, counts, n_tok)
        xs = _dispatch(x1r, dest, zstart, n_tok, n_rows)
        ys = _experts(xs, tile_expert, tile_valid, w1b, w3b, w2b, l, n_rows)
        x2d = _combine(ys, dest, x1r, rinfo, ln2_g[l].reshape(1, d), ln2_b[l].reshape(1, d), n_tok, alpha)
    return x2d.reshape(batch, seq, d)
```

```python
import functools

import jax
import jax.numpy as jnp
from jax import lax
from jax.experimental import pallas as pl
from jax.experimental.pallas import tpu as pltpu

f32 = jnp.float32
bf16 = jnp.bfloat16
i32 = jnp.int32

V7X_LANES = 128
V7X_SUBLANES = 8
V7X_VMEM_BYTES = 64 * 1024 * 1024

D_MODEL = 1024
CONV_WIDTH = 256
CONV_K = 3
SB_HEADS = 6
SB_HEAD_DIM = 64
SB_WIDTH = SB_HEADS * SB_HEAD_DIM
SB_BLOCK = 128
SB_QT = 512
SB_KT = 2 * SB_BLOCK
GLA_HEADS = 4
GLA_DV = 96
GLA_DK = 48
GLA_WIDTH = GLA_HEADS * GLA_DV
GLA_KEY_WIDTH = GLA_HEADS * GLA_DK
GLA_KEY_PAD = 256
GLA_GATE_RANK = 16
GLA_GATE_NORM = 16.0
GLA_CHUNK = 64
N_EXPERTS = 16
N_GROUPS = 4
EXPERTS_PER_GROUP = 4
D_EXPERT = 512
LN_EPS = 1e-5
RMS_EPS = 1e-6
LOG2E = 1.4426950408889634

SB_PROJ_WIDTH = 3 * SB_WIDTH
OFF_CONV = 0
OFF_GQ = 3 * CONV_WIDTH
OFF_GK = OFF_GQ + GLA_KEY_PAD
OFF_GV = OFF_GK + GLA_KEY_PAD
OFF_GG = OFF_GV + GLA_WIDTH
OFF_LR = OFF_GG + GLA_WIDTH
R_WIDTH = OFF_LR + V7X_LANES
P_WIDTH = SB_PROJ_WIDTH + R_WIDTH

ROW_TILE = V7X_SUBLANES
MIX_TILE = 1024
MOE_TOKEN_TILE = 512
MOE_ROW_TILE = 512
DMA_UNROLL = 8
COMBINE_CHUNKS = 8
INFO_ROWS = V7X_SUBLANES


def _vmem_params(est_bytes, semantics):
    limit = int(min(V7X_VMEM_BYTES - 6 * 1024 * 1024, max(est_bytes, 16 * 1024 * 1024)))
    return pltpu.CompilerParams(dimension_semantics=semantics, vmem_limit_bytes=limit)


def _neg_abs(x):
    bits = lax.bitcast_convert_type(x, jnp.uint32) | jnp.uint32(0x80000000)
    return lax.bitcast_convert_type(bits, f32)


def _rows_to_tiles_store(ref, val, n_rows):
    for j in range(D_MODEL // V7X_LANES):
        ref[pl.ds(j, n_rows, stride=ROW_TILE), :] = val[:, j * V7X_LANES:(j + 1) * V7X_LANES]


def _tiles_to_rows_load(ref, n_rows):
    return jnp.concatenate(
        [ref[pl.ds(j, n_rows, stride=ROW_TILE), :] for j in range(D_MODEL // V7X_LANES)], axis=1)


def _layer_norm(h, g, b):
    mu = jnp.mean(h, axis=-1, keepdims=True)
    c = h - mu
    var = jnp.mean(c * c, axis=-1, keepdims=True)
    return c * lax.rsqrt(var + LN_EPS) * g + b


def _in_proj_kernel(x_ref, w_ref, osb_ref, orest_ref):
    xb = x_ref[...].astype(bf16)
    n_chunk = 256
    for n0 in range(0, P_WIDTH, n_chunk):
        r = jnp.dot(xb, w_ref[:, n0:n0 + n_chunk], preferred_element_type=f32).astype(bf16)
        lo, hi = max(n0, 0), min(n0 + n_chunk, SB_PROJ_WIDTH)
        if hi > lo:
            osb_ref[:, lo:hi] = r[:, lo - n0:hi - n0]
        lo, hi = max(n0, SB_PROJ_WIDTH), n0 + n_chunk
        if hi > lo:
            orest_ref[:, lo - SB_PROJ_WIDTH:hi - SB_PROJ_WIDTH] = r[:, lo - n0:hi - n0]


def _in_proj(x2d, w):
    t = x2d.shape[0]
    tm = 512
    est = 2 * (tm * D_MODEL * 4 + D_MODEL * P_WIDTH * 2 + tm * P_WIDTH * 2) + 8 * 1024 * 1024
    return pl.pallas_call(
        _in_proj_kernel,
        grid=(t // tm,),
        in_specs=[pl.BlockSpec((tm, D_MODEL), lambda i: (i, 0)),
                  pl.BlockSpec((D_MODEL, P_WIDTH), lambda i: (0, 0))],
        out_specs=[pl.BlockSpec((tm, SB_PROJ_WIDTH), lambda i: (i, 0)),
                   pl.BlockSpec((tm, R_WIDTH), lambda i: (i, 0))],
        out_shape=[jax.ShapeDtypeStruct((t, SB_PROJ_WIDTH), bf16),
                   jax.ShapeDtypeStruct((t, R_WIDTH), bf16)],
        compiler_params=_vmem_params(est, ("arbitrary",)),
        name="in_proj",
    )(x2d, w)


def _sb_kernel(q_ref, k_ref, v_ref, o_ref, acc_ref, car_ref, *, seq):
    blk = SB_BLOCK
    qt = SB_QT
    n_pairs = SB_WIDTH // V7X_LANES
    klane = lax.broadcasted_iota(i32, (blk, V7X_LANES), 1)
    head_mask = [klane < SB_HEAD_DIM, klane >= SB_HEAD_DIM]
    wr = lax.broadcasted_iota(i32, (2 * blk, 2 * blk), 0)
    wc = lax.broadcasted_iota(i32, (2 * blk, 2 * blk), 1)
    suffix_w = jnp.where(((wr >= blk) == (wc >= blk)) & (wr >= wc), 1.0, 0.0).astype(bf16)
    def softplus2(z):
        return jnp.maximum(z, 0.0) + jnp.log2(1.0 + jnp.exp2(_neg_abs(z)))

    def lane_bcast(tot):
        n = tot.shape[0]
        return jnp.concatenate([jnp.broadcast_to(tot[:, 0:1], (n, blk)),
                                jnp.broadcast_to(tot[:, blk:blk + 1], (n, blk))], axis=1)

    def span(q_tiles, k0, diag_j):
        masked = diag_j is not None
        first = masked and diag_j == qt // SB_KT - 1
        r0 = diag_j * SB_KT if masked else 0
        nr = qt - r0
        if masked:
            ri = lax.broadcasted_iota(i32, (nr, 4 * blk), 0)
            ci = lax.broadcasted_iota(i32, (nr, 4 * blk), 1)
            diag_valid = (ci & (blk - 1)) + jnp.where(ci >= 2 * blk, blk, 0) < ri
        k_a = k_ref[pl.ds(k0, blk), :]
        k_b = k_ref[pl.ds(k0 + blk, blk), :]
        v_a = v_ref[pl.ds(k0, blk), :]
        v_b = v_ref[pl.ds(k0 + blk, blk), :]
        for p in range(n_pairs):
            sl = slice(p * V7X_LANES, (p + 1) * V7X_LANES)

            def stack(x_a, x_b):
                zero = jnp.zeros((blk, V7X_LANES), bf16)
                return jnp.concatenate([jnp.where(head_mask[0], x_a[:, sl], zero),
                                        jnp.where(head_mask[1], x_a[:, sl], zero),
                                        jnp.where(head_mask[0], x_b[:, sl], zero),
                                        jnp.where(head_mask[1], x_b[:, sl], zero)], axis=0)

            z = lax.dot_general(q_tiles[p][r0:], stack(k_a, k_b), (((1,), (1,)), ((), ())),
                                preferred_element_type=f32)
            sp = softplus2(z)
            if masked:
                sp = jnp.where(diag_valid, sp, 0.0)
            spb = sp.astype(bf16)
            tot_b = jnp.dot(spb[:, 2 * blk:], suffix_w, preferred_element_type=f32)
            if not first:
                tot_b = tot_b + car_ref[p, r0:, :]
            tot_a = jnp.dot(spb[:, :2 * blk], suffix_w, preferred_element_type=f32) + lane_bcast(tot_b)
            car_ref[p, r0:, :] = lane_bcast(tot_a)
            a = jnp.exp2(z - jnp.concatenate([tot_a, tot_b], axis=1))
            if masked:
                a = jnp.where(diag_valid, a, 0.0)
            av = jnp.dot(a.astype(bf16), stack(v_a, v_b), preferred_element_type=f32)
            if first:
                acc_ref[p, r0:, :] = av
                if r0 > 0:
                    acc_ref[p, :r0, :] = jnp.zeros((r0, V7X_LANES), f32)
                    car_ref[p, :r0, :] = jnp.zeros((r0, 2 * blk), f32)
            else:
                acc_ref[p, r0:, :] += av

    def q_tile(qi, carry):
        q0 = pl.multiple_of(qi * qt, qt)
        q_all = q_ref[pl.ds(q0, qt), :]
        q_tiles = [q_all[:, p * V7X_LANES:(p + 1) * V7X_LANES] for p in range(n_pairs)]
        for j in reversed(range(qt // SB_KT)):
            span(q_tiles, q0 + j * SB_KT, j)

        def k_span(i, c):
            span(q_tiles, pl.multiple_of(q0 - (i + 1) * SB_KT, SB_KT), None)
            return c

        lax.fori_loop(0, qi * (qt // SB_KT), k_span, 0)
        o_ref[pl.ds(q0, qt), :] = jnp.concatenate(
            [acc_ref[p] for p in range(n_pairs)], axis=1).astype(o_ref.dtype)
        return carry

    lax.fori_loop(0, seq // qt, q_tile, 0)


def _sb_attention(proj_sb, batch, seq):
    t = batch * seq
    n_pairs = SB_WIDTH // V7X_LANES
    est = 2 * 4 * seq * SB_WIDTH * 2 + n_pairs * SB_QT * 3 * V7X_LANES * 4 + 24 * 1024 * 1024
    return pl.pallas_call(
        functools.partial(_sb_kernel, seq=seq),
        grid=(batch,),
        in_specs=[pl.BlockSpec((seq, SB_WIDTH), lambda b: (b, 0)),
                  pl.BlockSpec((seq, SB_WIDTH), lambda b: (b, 1)),
                  pl.BlockSpec((seq, SB_WIDTH), lambda b: (b, 2))],
        out_specs=pl.BlockSpec((seq, SB_WIDTH), lambda b: (b, 0)),
        out_shape=jax.ShapeDtypeStruct((t, SB_WIDTH), bf16),
        scratch_shapes=[pltpu.VMEM((n_pairs, SB_QT, V7X_LANES), f32),
                        pltpu.VMEM((n_pairs, SB_QT, 2 * SB_BLOCK), f32)],
        compiler_params=_vmem_params(est, ("arbitrary",)),
        name="sb_attention",
    )(proj_sb, proj_sb, proj_sb)


def _route(sel, scores):
    c, s = sel, scores
    group_score = []
    for g in range(N_GROUPS):
        a, b, cc, d = c[4 * g:4 * g + 4]
        hi1, lo1 = jnp.maximum(a, b), jnp.minimum(a, b)
        hi2, lo2 = jnp.maximum(cc, d), jnp.minimum(cc, d)
        top1 = jnp.maximum(hi1, hi2)
        top2 = jnp.maximum(jnp.minimum(hi1, hi2), jnp.maximum(lo1, lo2))
        group_score.append(top1 + top2)
    best = group_score[0]
    gidx = jnp.zeros_like(best, dtype=i32)
    for g in range(1, N_GROUPS):
        better = group_score[g] > best
        gidx = jnp.where(better, g, gidx)
        best = jnp.where(better, group_score[g], best)
    vals, scs = [], []
    for j in range(EXPERTS_PER_GROUP):
        vj, sj = c[j], s[j]
        for g in range(1, N_GROUPS):
            pick = gidx == g
            vj = jnp.where(pick, c[4 * g + j], vj)
            sj = jnp.where(pick, s[4 * g + j], sj)
        vals.append(vj)
        scs.append(sj)
    m1, i1, w1 = vals[0], jnp.zeros_like(gidx), scs[0]
    for j in range(1, EXPERTS_PER_GROUP):
        better = vals[j] > m1
        m1 = jnp.where(better, vals[j], m1)
        i1 = jnp.where(better, j, i1)
        w1 = jnp.where(better, scs[j], w1)
    m2 = jnp.full_like(m1, -jnp.inf)
    i2 = jnp.full_like(i1, -1)
    w2 = jnp.zeros_like(w1)
    for j in range(EXPERTS_PER_GROUP):
        better = (i1 != j) & ((vals[j] > m2) | (i2 < 0))
        m2 = jnp.where(better, vals[j], m2)
        i2 = jnp.where(better, j, i2)
        w2 = jnp.where(better, scs[j], w2)
    denom = w1 + w2
    return w1 / denom, w2 / denom, gidx * EXPERTS_PER_GROUP + i1, gidx * EXPERTS_PER_GROUP + i2


def _mix_kernel(proj_ref, ysb_ref, x_ref, convw_ref, convb_ref, wg2_ref, bg_ref, normw_ref,
                wout_ref, lng_ref, lnb_ref, rwt_ref, rbt_ref,
                x1r_ref, rinfo_ref, counts_ref,
                halo_ref, state_ref, cnt_ref, *, ts, alpha):
    t_idx = pl.program_id(1)
    kp = GLA_KEY_PAD
    ck = GLA_CHUNK

    @pl.when((pl.program_id(0) == 0) & (t_idx == 0))
    def _():
        cnt_ref[...] = jnp.zeros_like(cnt_ref)

    @pl.when(t_idx == 0)
    def _():
        halo_ref[...] = jnp.zeros_like(halo_ref)
        state_ref[...] = jnp.zeros_like(state_ref)

    c_h = proj_ref[:, OFF_CONV:OFF_CONV + CONV_WIDTH].astype(f32)
    c_b = proj_ref[:, OFF_CONV + CONV_WIDTH:OFF_CONV + 2 * CONV_WIDTH].astype(f32)
    c_c = proj_ref[:, OFF_CONV + 2 * CONV_WIDTH:OFF_CONV + 3 * CONV_WIDTH].astype(f32)
    u = c_c * c_h
    ucat = jnp.concatenate([halo_ref[...], u], axis=0)
    halo_ref[...] = u[ts - V7X_SUBLANES:, :]
    u1 = ucat[V7X_SUBLANES - 1:V7X_SUBLANES - 1 + ts, :]
    u2 = ucat[V7X_SUBLANES - 2:V7X_SUBLANES - 2 + ts, :]
    y = convb_ref[...] + convw_ref[0:1, :] * u
    y = y + convw_ref[1:2, :] * u1
    y = y + convw_ref[2:3, :] * u2
    y_conv = c_b * y

    lr = proj_ref[:, OFF_LR:OFF_LR + V7X_LANES]
    pre = jnp.dot(lr, wg2_ref[...], preferred_element_type=f32) + bg_ref[...]
    log_a = (jnp.minimum(pre, 0.0) - jnp.log1p(jnp.exp(_neg_abs(pre)))) * (1.0 / GLA_GATE_NORM)
    chunk_shift = ck.bit_length() - 1
    dck = 2 * ck
    ir = lax.broadcasted_iota(i32, (dck, dck), 0)
    ic = lax.broadcasted_iota(i32, (dck, dck), 1)
    same_chunk = (ir >> chunk_shift) == (ic >> chunk_shift)
    intra_mask = same_chunk & (ic <= ir)
    cum_w = jnp.where(intra_mask, 1.0, 0.0).astype(bf16)
    tot_w = jnp.where(same_chunk, 1.0, 0.0).astype(bf16)
    cum_tot_w = jnp.concatenate([jnp.concatenate([cum_w, cum_w], axis=1),
                                 jnp.concatenate([tot_w, tot_w], axis=1)], axis=0)
    la_hi = log_a.astype(bf16)
    la_lo = (log_a - la_hi.astype(f32)).astype(bf16)
    bcum, blast = [], []
    for j in range(ts // dck):
        rows = slice(j * dck, (j + 1) * dck)
        r = jnp.dot(cum_tot_w, jnp.concatenate([la_hi[rows], la_lo[rows]], axis=0),
                    preferred_element_type=f32)
        bcum.append(r[:dck])
        blast.append(r[dck:])
    bcum = jnp.concatenate(bcum, axis=0)
    blast = jnp.concatenate(blast, axis=0)
    qf = proj_ref[:, OFF_GQ:OFF_GQ + kp].astype(f32)
    kf = proj_ref[:, OFF_GK:OFF_GK + kp].astype(f32)
    vb = proj_ref[:, OFF_GV:OFF_GV + GLA_WIDTH]
    q_in = ((qf * (GLA_DK ** -0.5)) * jnp.exp(bcum)).astype(bf16)
    k_in = (kf * jnp.exp(-bcum)).astype(bf16)
    k_te = (kf * jnp.exp(blast - bcum)).astype(bf16)
    decay = jnp.exp(blast)

    klane = lax.broadcasted_iota(i32, (dck, kp), 1)
    vlane = lax.broadcasted_iota(i32, (dck, GLA_WIDTH), 1)
    kmask = [(klane >= h * GLA_DK) & (klane < (h + 1) * GLA_DK) for h in range(GLA_HEADS)]
    vmask = [(vlane >= h * GLA_DV) & (vlane < (h + 1) * GLA_DV) for h in range(GLA_HEADS)]
    o_intra = []
    for j in range(ts // dck):
        rows = slice(j * dck, (j + 1) * dck)
        q_j, k_j, v_j = q_in[rows], k_in[rows], vb[rows]
        o_j = jnp.zeros((dck, GLA_WIDTH), f32)
        for h in range(GLA_HEADS):
            qh = jnp.where(kmask[h], q_j, jnp.zeros_like(q_j))
            sc = lax.dot_general(qh, k_j, (((1,), (1,)), ((), ())), preferred_element_type=f32)
            sc = jnp.where(intra_mask, sc, 0.0).astype(bf16)
            o_j = jnp.where(vmask[h], jnp.dot(sc, v_j, preferred_element_type=f32), o_j)
        o_intra.append(o_j)
    o_intra = jnp.concatenate(o_intra, axis=0)

    srow = lax.broadcasted_iota(i32, (GLA_WIDTH, kp), 0)
    scol = lax.broadcasted_iota(i32, (GLA_WIDTH, kp), 1)
    blockdiag = jnp.zeros((GLA_WIDTH, kp), jnp.bool_)
    for h in range(GLA_HEADS):
        blockdiag = blockdiag | ((srow >= h * GLA_DV) & (srow < (h + 1) * GLA_DV)
                                 & (scol >= h * GLA_DK) & (scol < (h + 1) * GLA_DK))
    kvt = [lax.dot_general(vb[c * ck:(c + 1) * ck], k_te[c * ck:(c + 1) * ck], (((0,), (0,)), ((), ())),
                           preferred_element_type=f32) for c in range(ts // ck)]
    state = state_ref[...]
    o_inter = []
    for c in range(ts // ck):
        o_inter.append(lax.dot_general(q_in[c * ck:(c + 1) * ck], state.astype(bf16),
                                       (((1,), (1,)), ((), ())), preferred_element_type=f32))
        state = decay[c * ck:c * ck + 1, :] * state + jnp.where(blockdiag, kvt[c], 0.0)
    state_ref[...] = state
    o_all = o_intra + jnp.concatenate(o_inter, axis=0)

    o_sq = o_all * o_all
    vlane_t = lax.broadcasted_iota(i32, (ts, GLA_WIDTH), 1)
    inv = jnp.zeros((ts, GLA_WIDTH), f32)
    for h in range(GLA_HEADS):
        hm = (vlane_t >= h * GLA_DV) & (vlane_t < (h + 1) * GLA_DV)
        ms = jnp.sum(jnp.where(hm, o_sq, 0.0), axis=-1, keepdims=True) * (1.0 / GLA_DV)
        inv = jnp.where(hm, lax.rsqrt(ms + RMS_EPS), inv)
    gg = proj_ref[:, OFF_GG:OFF_GG + GLA_WIDTH].astype(f32)
    y_gla = (o_all * inv * normw_ref[...]) * (gg * (1.0 / (1.0 + jnp.exp(-gg))))

    mix = jnp.concatenate([y_conv.astype(bf16), ysb_ref[...], y_gla.astype(bf16)], axis=1)
    proj_out = jnp.dot(mix, wout_ref[...], preferred_element_type=f32)
    x1 = _layer_norm(alpha * x_ref[...] + proj_out, lng_ref[...], lnb_ref[...])
    _rows_to_tiles_store(x1r_ref, x1, ts)

    logits = lax.dot_general(rwt_ref[...], x1.astype(bf16), (((1,), (1,)), ((), ())),
                             preferred_element_type=f32)
    scores = 1.0 / (1.0 + jnp.exp(-logits))
    sel = scores + rbt_ref[...]
    g0, g1, e0, e1 = _route([sel[e:e + 1, :] for e in range(N_EXPERTS)],
                            [scores[e:e + 1, :] for e in range(N_EXPERTS)])
    eid = lax.broadcasted_iota(i32, (N_EXPERTS, ts), 0)
    pick0 = eid == e0
    pick1 = eid == e1
    onehot = jnp.where(pick0 | pick1, 1.0, 0.0)
    rr = lax.broadcasted_iota(i32, (ts, ts), 0)
    cc = lax.broadcasted_iota(i32, (ts, ts), 1)
    before = jnp.dot(onehot.astype(bf16), jnp.where(rr < cc, 1.0, 0.0).astype(bf16),
                     preferred_element_type=f32) + cnt_ref[...]
    rank0 = jnp.sum(jnp.where(pick0, before, 0.0), axis=0, keepdims=True)
    rank1 = jnp.sum(jnp.where(pick1, before, 0.0), axis=0, keepdims=True)
    new_cnt = cnt_ref[...] + jnp.sum(onehot, axis=1, keepdims=True)
    cnt_ref[...] = new_cnt
    counts_ref[...] = new_cnt[:, :V7X_LANES]
    zero = jnp.zeros_like(g0)
    rinfo_ref[...] = jnp.concatenate(
        [g0, g1, e0.astype(f32), e1.astype(f32), rank0, rank1, zero, zero], axis=0)


def _mix(proj, ysb, x2d, lw, batch, seq, alpha):
    t = batch * seq
    ts = MIX_TILE
    nt = seq // ts
    row = lambda b, s: (b * nt + s, 0)
    const = lambda b, s: (0, 0)
    est = (2 * (ts * R_WIDTH * 2 + ts * SB_WIDTH * 2 + 2 * ts * D_MODEL * 4 + D_MODEL * D_MODEL * 2)
           + 28 * 1024 * 1024)
    return pl.pallas_call(
        functools.partial(_mix_kernel, ts=ts, alpha=alpha),
        grid=(batch, nt),
        in_specs=[pl.BlockSpec((ts, R_WIDTH), row),
                  pl.BlockSpec((ts, SB_WIDTH), row),
                  pl.BlockSpec((ts, D_MODEL), row),
                  pl.BlockSpec((V7X_SUBLANES, CONV_WIDTH), const),
                  pl.BlockSpec((1, CONV_WIDTH), const),
                  pl.BlockSpec((V7X_LANES, GLA_KEY_PAD), const),
                  pl.BlockSpec((1, GLA_KEY_PAD), const),
                  pl.BlockSpec((1, GLA_WIDTH), const),
                  pl.BlockSpec((D_MODEL, D_MODEL), const),
                  pl.BlockSpec((1, D_MODEL), const),
                  pl.BlockSpec((1, D_MODEL), const),
                  pl.BlockSpec((N_EXPERTS, D_MODEL), const),
                  pl.BlockSpec((N_EXPERTS, ts), const)],
        out_specs=[pl.BlockSpec((ts * ROW_TILE, V7X_LANES), row),
                   pl.BlockSpec((INFO_ROWS, ts), lambda b, s: (0, b * nt + s)),
                   pl.BlockSpec((N_EXPERTS, V7X_LANES), const)],
        out_shape=[jax.ShapeDtypeStruct((t * ROW_TILE, V7X_LANES), f32),
                   jax.ShapeDtypeStruct((INFO_ROWS, t), f32),
                   jax.ShapeDtypeStruct((N_EXPERTS, V7X_LANES), f32)],
        scratch_shapes=[pltpu.VMEM((V7X_SUBLANES, CONV_WIDTH), f32),
                        pltpu.VMEM((GLA_WIDTH, GLA_KEY_PAD), f32),
                        pltpu.VMEM((N_EXPERTS, ts), f32)],
        compiler_params=_vmem_params(est, ("arbitrary", "arbitrary")),
        name="mix_ln_router",
    )(proj, ysb, x2d, lw["conv_w"], lw["conv_b"], lw["wg2"], lw["bg"], lw["normw"], lw["w_out"],
      lw["ln1_g"], lw["ln1_b"], lw["rwt"], lw["rbt"])


def _dest_plan(rinfo, counts, n_tok):
    tm = MOE_ROW_TILE
    n_rows = 2 * n_tok + N_EXPERTS * tm
    cnt = counts[:, 0].astype(i32)
    padded = ((cnt + tm - 1) // tm) * tm
    ends = jnp.cumsum(padded)
    starts = ends - padded
    ids = jnp.arange(N_EXPERTS, dtype=i32)[None, :]
    e0 = rinfo[2].astype(i32)
    e1 = rinfo[3].astype(i32)
    dest0 = rinfo[4].astype(i32) + jnp.sum(jnp.where(e0[:, None] == ids, starts[None, :], 0), axis=1)
    dest1 = rinfo[5].astype(i32) + jnp.sum(jnp.where(e1[:, None] == ids, starts[None, :], 0), axis=1)
    nb = n_tok // MOE_TOKEN_TILE
    dest = jnp.concatenate([dest0.reshape(nb, 1, MOE_TOKEN_TILE), dest1.reshape(nb, 1, MOE_TOKEN_TILE)], axis=2)
    tile_start = jnp.arange(n_rows // tm, dtype=i32) * tm
    tile_expert = jnp.minimum(jnp.sum((tile_start[:, None] >= ends[None, :]).astype(i32), axis=1),
                              N_EXPERTS - 1).astype(i32)
    tile_valid = (tile_start < ends[-1]).astype(i32)
    tile_first = (jnp.sum((tile_start[:, None] == starts[None, :]).astype(i32), axis=1) > 0).astype(i32) * tile_valid
    tail = ends[-1] + jnp.arange(N_EXPERTS, dtype=i32) * tm
    zstart = jnp.concatenate([jnp.where(padded > 0, ends - tm, -1),
                              jnp.where(tail < n_rows, tail, -1)]).astype(i32)
    return dest, zstart, tile_expert, tile_valid, tile_first, n_rows


def _dispatch_kernel(zstart_ref, dest_ref, x1r_ref, xs_hbm, zbuf, sem, zsem, *, tt):
    @pl.when(pl.program_id(0) == 0)
    def _():
        zbuf[...] = jnp.zeros_like(zbuf)

        def zero_copy(e):
            z0 = pl.multiple_of(jnp.maximum(zstart_ref[e], 0) * ROW_TILE, ROW_TILE)
            return pltpu.make_async_copy(zbuf, xs_hbm.at[pl.ds(z0, MOE_ROW_TILE * ROW_TILE), :], zsem)

        for e in range(2 * N_EXPERTS):
            @pl.when(zstart_ref[e] >= 0)
            def _():
                zero_copy(e).start()
        for e in range(2 * N_EXPERTS):
            @pl.when(zstart_ref[e] >= 0)
            def _():
                zero_copy(e).wait()

    def row_copy(r, j):
        d = dest_ref[0, 0, j]
        return pltpu.make_async_copy(
            x1r_ref.at[pl.ds(pl.multiple_of(r * ROW_TILE, ROW_TILE), ROW_TILE), :],
            xs_hbm.at[pl.ds(pl.multiple_of(d * ROW_TILE, ROW_TILE), ROW_TILE), :], sem)

    def issue(g, c):
        for u in range(DMA_UNROLL):
            r = g * DMA_UNROLL + u
            row_copy(r, r).start(priority=0)
            row_copy(r, tt + r).start(priority=1)
        return c

    lax.fori_loop(0, tt // DMA_UNROLL, issue, 0)
    whole = pltpu.make_async_copy(x1r_ref, xs_hbm.at[pl.ds(0, tt * ROW_TILE), :], sem)
    whole.wait()
    whole.wait()


def _dispatch(x1r, dest, zstart, n_tok, n_rows):
    tt = MOE_TOKEN_TILE
    est = 2 * tt * D_MODEL * 4 + MOE_ROW_TILE * D_MODEL * 4 + 8 * 1024 * 1024
    grid_spec = pltpu.PrefetchScalarGridSpec(
        num_scalar_prefetch=1,
        grid=(n_tok // tt,),
        in_specs=[pl.BlockSpec((1, 1, 2 * tt), lambda i, z: (i, 0, 0), memory_space=pltpu.SMEM),
                  pl.BlockSpec((tt * ROW_TILE, V7X_LANES), lambda i, z: (i, 0))],
        out_specs=pl.BlockSpec(memory_space=pl.ANY),
        scratch_shapes=[pltpu.VMEM((MOE_ROW_TILE * ROW_TILE, V7X_LANES), f32),
                        pltpu.SemaphoreType.DMA(()),
                        pltpu.SemaphoreType.DMA(())],
    )
    return pl.pallas_call(
        functools.partial(_dispatch_kernel, tt=tt),
        grid_spec=grid_spec,
        out_shape=jax.ShapeDtypeStruct((n_rows * ROW_TILE, V7X_LANES), f32),
        compiler_params=_vmem_params(est, ("arbitrary",)),
        name="dispatch",
    )(zstart, dest, x1r)


def _expert_kernel(texp_ref, tvalid_ref, tfirst_ref, xs_ref, w1_ref, w3_ref, w2_ref, ys_ref,
                   w1b, w3b, w2b, *, tm):
    i = pl.program_id(0)

    @pl.when(tfirst_ref[i] > 0)
    def _():
        w1b[...] = w1_ref[0].astype(bf16)
        w3b[...] = w3_ref[0].astype(bf16)
        w2b[...] = w2_ref[0].astype(bf16)

    @pl.when(tvalid_ref[i] > 0)
    def _():
        xb = _tiles_to_rows_load(xs_ref, tm).astype(bf16)
        a = jnp.dot(xb, w1b[...], preferred_element_type=f32)
        b = jnp.dot(xb, w3b[...], preferred_element_type=f32)
        h = (a * (1.0 / (1.0 + jnp.exp(-a)))) * b
        o = jnp.dot(h.astype(bf16), w2b[...], preferred_element_type=f32)
        _rows_to_tiles_store(ys_ref, o, tm)

    @pl.when(tvalid_ref[i] == 0)
    def _():
        ys_ref[...] = jnp.zeros_like(ys_ref)


def _experts(xs, tile_expert, tile_valid, tile_first, w1, w3, w2, layer, n_rows):
    tm = MOE_ROW_TILE
    base = layer * N_EXPERTS
    blk = (tm * ROW_TILE, V7X_LANES)
    est = 2 * (3 * D_MODEL * D_EXPERT * 4 + 2 * tm * D_MODEL * 4) + 3 * D_MODEL * D_EXPERT * 2 + 16 * 1024 * 1024
    grid_spec = pltpu.PrefetchScalarGridSpec(
        num_scalar_prefetch=3,
        grid=(n_rows // tm,),
        in_specs=[pl.BlockSpec(blk, lambda i, te, tv, tf: (i, 0)),
                  pl.BlockSpec((1, D_MODEL, D_EXPERT), lambda i, te, tv, tf: (base + te[i], 0, 0)),
                  pl.BlockSpec((1, D_MODEL, D_EXPERT), lambda i, te, tv, tf: (base + te[i], 0, 0)),
                  pl.BlockSpec((1, D_EXPERT, D_MODEL), lambda i, te, tv, tf: (base + te[i], 0, 0))],
        out_specs=pl.BlockSpec(blk, lambda i, te, tv, tf: (i, 0)),
        scratch_shapes=[pltpu.VMEM((D_MODEL, D_EXPERT), bf16),
                        pltpu.VMEM((D_MODEL, D_EXPERT), bf16),
                        pltpu.VMEM((D_EXPERT, D_MODEL), bf16)],
    )
    return pl.pallas_call(
        functools.partial(_expert_kernel, tm=tm),
        grid_spec=grid_spec,
        out_shape=jax.ShapeDtypeStruct((n_rows * ROW_TILE, V7X_LANES), f32),
        compiler_params=_vmem_params(est, ("arbitrary",)),
        name="experts",
    )(tile_expert, tile_valid, tile_first, xs, w1, w3, w2)


def _combine_kernel(dest_ref, dest_next_ref, ys_hbm, x1r_ref, rinfo_ref, g_ref, b_ref, o_ref,
                    buf, sem, *, tt, alpha):
    i = pl.program_id(0)
    n = pl.num_programs(0)
    slot = lax.rem(i, 2)
    nslot = 1 - slot

    def row_copy(d_ref, s, j):
        d = d_ref[0, 0, j]
        return pltpu.make_async_copy(
            ys_hbm.at[pl.ds(pl.multiple_of(d * ROW_TILE, ROW_TILE), ROW_TILE), :],
            buf.at[s, pl.ds(pl.multiple_of(j * ROW_TILE, ROW_TILE), ROW_TILE), :], sem.at[s])

    def wait_slot(s):
        pltpu.make_async_copy(ys_hbm.at[pl.ds(0, 2 * tt * ROW_TILE), :], buf.at[s], sem.at[s]).wait()

    @pl.when(i == 0)
    def _():
        def issue(g, c):
            for u in range(DMA_UNROLL):
                row_copy(dest_ref, 0, g * DMA_UNROLL + u).start(priority=u % 2)
            return c
        lax.fori_loop(0, 2 * tt // DMA_UNROLL, issue, 0)

    wait_slot(slot)
    gates = jnp.transpose(rinfo_ref[...])
    tc = tt // COMBINE_CHUNKS
    per_chunk = 2 * tt // COMBINE_CHUNKS
    for c in range(COMBINE_CHUNKS):
        for j in range(c * per_chunk, (c + 1) * per_chunk):
            row_copy(dest_next_ref, nslot, j).start(priority=j % 2)
        r0 = c * tc * ROW_TILE
        o0 = _tiles_to_rows_load(buf.at[slot, pl.ds(r0, tc * ROW_TILE), :], tc)
        o1 = _tiles_to_rows_load(buf.at[slot, pl.ds(tt * ROW_TILE + r0, tc * ROW_TILE), :], tc)
        x1 = _tiles_to_rows_load(x1r_ref.at[pl.ds(r0, tc * ROW_TILE), :], tc)
        g = gates[c * tc:(c + 1) * tc]
        ffn = g[:, 0:1] * o0 + g[:, 1:2] * o1
        o_ref[c * tc:(c + 1) * tc, :] = _layer_norm(alpha * x1 + ffn, g_ref[...], b_ref[...])

    @pl.when(i == n - 1)
    def _():
        wait_slot(nslot)


def _combine(ys, dest, x1r, rinfo, ln_g, ln_b, n_tok, alpha):
    tt = MOE_TOKEN_TILE
    nb = n_tok // tt
    est = 2 * (tt * D_MODEL * 4 * 2 + tt * V7X_LANES * 4) + 2 * 2 * tt * D_MODEL * 4 + 24 * 1024 * 1024
    return pl.pallas_call(
        functools.partial(_combine_kernel, tt=tt, alpha=alpha),
        grid=(nb,),
        in_specs=[pl.BlockSpec((1, 1, 2 * tt), lambda i: (i, 0, 0), memory_space=pltpu.SMEM),
                  pl.BlockSpec((1, 1, 2 * tt), lambda i: (jnp.minimum(i + 1, nb - 1), 0, 0),
                               memory_space=pltpu.SMEM),
                  pl.BlockSpec(memory_space=pl.ANY),
                  pl.BlockSpec((tt * ROW_TILE, V7X_LANES), lambda i: (i, 0)),
                  pl.BlockSpec((INFO_ROWS, tt), lambda i: (0, i)),
                  pl.BlockSpec((1, D_MODEL), lambda i: (0, 0)),
                  pl.BlockSpec((1, D_MODEL), lambda i: (0, 0))],
        out_specs=pl.BlockSpec((tt, D_MODEL), lambda i: (i, 0)),
        out_shape=jax.ShapeDtypeStruct((n_tok, D_MODEL), f32),
        scratch_shapes=[pltpu.VMEM((2, 2 * tt * ROW_TILE, V7X_LANES), f32),
                        pltpu.SemaphoreType.DMA((2,))],
        compiler_params=_vmem_params(est, ("arbitrary",)),
        name="combine_ln",
    )(dest, dest, ys, x1r, rinfo, ln_g, ln_b)


def _prep_w_in(w_in_l):
    o = 3 * CONV_WIDTH
    conv = w_in_l[:, :o]
    sbq = w_in_l[:, o:o + SB_WIDTH] * (SB_HEAD_DIM ** -0.5 * LOG2E)
    sbk = w_in_l[:, o + SB_WIDTH:o + 2 * SB_WIDTH]
    sbv = w_in_l[:, o + 2 * SB_WIDTH:o + 3 * SB_WIDTH]
    o = o + 3 * SB_WIDTH
    gq = w_in_l[:, o:o + GLA_KEY_WIDTH]
    gk = w_in_l[:, o + GLA_KEY_WIDTH:o + 2 * GLA_KEY_WIDTH]
    o = o + 2 * GLA_KEY_WIDTH
    gv = w_in_l[:, o:o + GLA_WIDTH]
    gg = w_in_l[:, o + GLA_WIDTH:o + 2 * GLA_WIDTH]
    lr = w_in_l[:, o + 2 * GLA_WIDTH:]
    zk = jnp.zeros((D_MODEL, GLA_KEY_PAD - GLA_KEY_WIDTH), f32)
    zl = jnp.zeros((D_MODEL, V7X_LANES - GLA_GATE_RANK), f32)
    return jnp.concatenate([sbq, sbk, sbv, conv, gq, zk, gk, zk, gv, gg, lr, zl], axis=1).astype(bf16)


def kernel(x, w_in, conv_w, conv_b, gla_w_gate2, gla_b_gate, gla_norm_w, w_out, ln1_g, ln1_b,
           router_w, router_bias, w1, w3, w2, ln2_g, ln2_b):
    batch, seq, d = x.shape
    depth = w_in.shape[0]
    n_tok = batch * seq
    alpha = (2 * depth) ** 0.25

    w1f = w1.reshape(depth * N_EXPERTS, D_MODEL, D_EXPERT)
    w3f = w3.reshape(depth * N_EXPERTS, D_MODEL, D_EXPERT)
    w2f = w2.reshape(depth * N_EXPERTS, D_EXPERT, D_MODEL)
    rwt = router_w.T.astype(bf16)
    rbt = jnp.broadcast_to(router_bias.reshape(N_EXPERTS, 1), (N_EXPERTS, MIX_TILE))

    x2d = x.reshape(n_tok, d)
    for l in range(depth):
        lw = dict(
            conv_w=jnp.pad(conv_w[l], ((0, V7X_SUBLANES - CONV_K), (0, 0))),
            conv_b=conv_b[l].reshape(1, CONV_WIDTH),
            wg2=jnp.pad(gla_w_gate2[l], ((0, V7X_LANES - GLA_GATE_RANK),
                                         (0, GLA_KEY_PAD - GLA_KEY_WIDTH))).astype(bf16),
            bg=jnp.pad(gla_b_gate[l].reshape(1, GLA_KEY_WIDTH), ((0, 0), (0, GLA_KEY_PAD - GLA_KEY_WIDTH))),
            normw=jnp.tile(gla_norm_w[l], GLA_HEADS).reshape(1, GLA_WIDTH),
            w_out=w_out[l].astype(bf16),
            ln1_g=ln1_g[l].reshape(1, d), ln1_b=ln1_b[l].reshape(1, d),
            rwt=rwt, rbt=rbt)
        proj_sb, proj_rest = _in_proj(x2d, _prep_w_in(w_in[l]))
        ysb = _sb_attention(proj_sb, batch, seq)
        x1r, rinfo, counts = _mix(proj_rest, ysb, x2d, lw, batch, seq, alpha)
        dest, zstart, tile_expert, tile_valid, tile_first, n_rows = _dest_plan(rinfo, counts, n_tok)
        xs = _dispatch(x1r, dest, zstart, n_tok, n_rows)
        ys = _experts(xs, tile_expert, tile_valid, tile_first, w1f, w3f, w2f, l, n_rows)
        x2d = _combine(ys, dest, x1r, rinfo, ln2_g[l].reshape(1, d), ln2_b[l].reshape(1, d), n_tok, alpha)
    return x2d.reshape(batch, seq, d)
```

```python
import functools

import jax
import jax.numpy as jnp
from jax import lax
from jax.experimental import pallas as pl
from jax.experimental.pallas import tpu as pltpu

f32 = jnp.float32
bf16 = jnp.bfloat16
i32 = jnp.int32

V7X_LANES = 128
V7X_SUBLANES = 8
V7X_VMEM_BYTES = 64 * 1024 * 1024

D_MODEL = 1024
CONV_WIDTH = 256
CONV_K = 3
SB_HEADS = 6
SB_HEAD_DIM = 64
SB_WIDTH = SB_HEADS * SB_HEAD_DIM
SB_BLOCK = 128
SB_QT = 512
SB_KT = 2 * SB_BLOCK
GLA_HEADS = 4
GLA_DV = 96
GLA_DK = 48
GLA_WIDTH = GLA_HEADS * GLA_DV
GLA_KEY_WIDTH = GLA_HEADS * GLA_DK
GLA_KEY_PAD = 256
GLA_GATE_RANK = 16
GLA_GATE_NORM = 16.0
GLA_CHUNK = 64
N_EXPERTS = 16
N_GROUPS = 4
EXPERTS_PER_GROUP = 4
D_EXPERT = 512
LN_EPS = 1e-5
RMS_EPS = 1e-6
LOG2E = 1.4426950408889634

SB_PROJ_WIDTH = 3 * SB_WIDTH
OFF_CONV = 0
OFF_GQ = 3 * CONV_WIDTH
OFF_GK = OFF_GQ + GLA_KEY_PAD
OFF_GV = OFF_GK + GLA_KEY_PAD
OFF_GG = OFF_GV + GLA_WIDTH
OFF_LR = OFF_GG + GLA_WIDTH
R_WIDTH = OFF_LR + V7X_LANES
P_WIDTH = SB_PROJ_WIDTH + R_WIDTH

ROW_TILE = V7X_SUBLANES
MIX_TILE = 1024
MOE_TOKEN_TILE = 512
MOE_ROW_TILE = 512
DMA_UNROLL = 8
COMBINE_CHUNKS = 8
INFO_ROWS = V7X_SUBLANES


def _vmem_params(est_bytes, semantics):
    limit = int(min(V7X_VMEM_BYTES - 6 * 1024 * 1024, max(est_bytes, 16 * 1024 * 1024)))
    return pltpu.CompilerParams(dimension_semantics=semantics, vmem_limit_bytes=limit)


def _neg_abs(x):
    bits = lax.bitcast_convert_type(x, jnp.uint32) | jnp.uint32(0x80000000)
    return lax.bitcast_convert_type(bits, f32)


def _rows_to_tiles_store(ref, val, n_rows):
    for j in range(D_MODEL // V7X_LANES):
        ref[pl.ds(j, n_rows, stride=ROW_TILE), :] = val[:, j * V7X_LANES:(j + 1) * V7X_LANES]


def _tiles_to_rows_load(ref, n_rows):
    return jnp.concatenate(
        [ref[pl.ds(j, n_rows, stride=ROW_TILE), :] for j in range(D_MODEL // V7X_LANES)], axis=1)


def _layer_norm(h, g, b):
    mu = jnp.mean(h, axis=-1, keepdims=True)
    c = h - mu
    var = jnp.mean(c * c, axis=-1, keepdims=True)
    return c * lax.rsqrt(var + LN_EPS) * g + b


def _in_proj_kernel(x_ref, w_ref, osb_ref, orest_ref):
    xb = x_ref[...].astype(bf16)
    n_chunk = 256
    for n0 in range(0, P_WIDTH, n_chunk):
        r = jnp.dot(xb, w_ref[:, n0:n0 + n_chunk], preferred_element_type=f32).astype(bf16)
        lo, hi = max(n0, 0), min(n0 + n_chunk, SB_PROJ_WIDTH)
        if hi > lo:
            osb_ref[:, lo:hi] = r[:, lo - n0:hi - n0]
        lo, hi = max(n0, SB_PROJ_WIDTH), n0 + n_chunk
        if hi > lo:
            orest_ref[:, lo - SB_PROJ_WIDTH:hi - SB_PROJ_WIDTH] = r[:, lo - n0:hi - n0]


def _in_proj(x2d, w):
    t = x2d.shape[0]
    tm = 512
    est = 2 * (tm * D_MODEL * 4 + D_MODEL * P_WIDTH * 2 + tm * P_WIDTH * 2) + 8 * 1024 * 1024
    return pl.pallas_call(
        _in_proj_kernel,
        grid=(t // tm,),
        in_specs=[pl.BlockSpec((tm, D_MODEL), lambda i: (i, 0)),
                  pl.BlockSpec((D_MODEL, P_WIDTH), lambda i: (0, 0))],
        out_specs=[pl.BlockSpec((tm, SB_PROJ_WIDTH), lambda i: (i, 0)),
                   pl.BlockSpec((tm, R_WIDTH), lambda i: (i, 0))],
        out_shape=[jax.ShapeDtypeStruct((t, SB_PROJ_WIDTH), bf16),
                   jax.ShapeDtypeStruct((t, R_WIDTH), bf16)],
        compiler_params=_vmem_params(est, ("arbitrary",)),
        name="in_proj",
    )(x2d, w)


def _sb_kernel(q_ref, k_ref, v_ref, o_ref, acc_ref, car_ref, *, seq):
    blk = SB_BLOCK
    qt = SB_QT
    n_pairs = SB_WIDTH // V7X_LANES
    klane = lax.broadcasted_iota(i32, (blk, V7X_LANES), 1)
    head_mask = [klane < SB_HEAD_DIM, klane >= SB_HEAD_DIM]
    wr = lax.broadcasted_iota(i32, (2 * blk, 2 * blk), 0)
    wc = lax.broadcasted_iota(i32, (2 * blk, 2 * blk), 1)
    suffix_w = jnp.where(((wr >= blk) == (wc >= blk)) & (wr >= wc), 1.0, 0.0).astype(bf16)
    def softplus2(z):
        return jnp.maximum(z, 0.0) + jnp.log2(1.0 + jnp.exp2(_neg_abs(z)))

    def lane_bcast(tot):
        n = tot.shape[0]
        return jnp.concatenate([jnp.broadcast_to(tot[:, 0:1], (n, blk)),
                                jnp.broadcast_to(tot[:, blk:blk + 1], (n, blk))], axis=1)

    def span(q_tiles, k0, diag_j):
        masked = diag_j is not None
        first = masked and diag_j == qt // SB_KT - 1
        r0 = diag_j * SB_KT if masked else 0
        nr = qt - r0
        if masked:
            ri = lax.broadcasted_iota(i32, (nr, 4 * blk), 0)
            ci = lax.broadcasted_iota(i32, (nr, 4 * blk), 1)
            diag_valid = (ci & (blk - 1)) + jnp.where(ci >= 2 * blk, blk, 0) < ri
        k_a = k_ref[pl.ds(k0, blk), :]
        k_b = k_ref[pl.ds(k0 + blk, blk), :]
        v_a = v_ref[pl.ds(k0, blk), :]
        v_b = v_ref[pl.ds(k0 + blk, blk), :]
        for p in range(n_pairs):
            sl = slice(p * V7X_LANES, (p + 1) * V7X_LANES)

            def stack(x_a, x_b):
                zero = jnp.zeros((blk, V7X_LANES), bf16)
                return jnp.concatenate([jnp.where(head_mask[0], x_a[:, sl], zero),
                                        jnp.where(head_mask[1], x_a[:, sl], zero),
                                        jnp.where(head_mask[0], x_b[:, sl], zero),
                                        jnp.where(head_mask[1], x_b[:, sl], zero)], axis=0)

            z = lax.dot_general(q_tiles[p][r0:], stack(k_a, k_b), (((1,), (1,)), ((), ())),
                                preferred_element_type=f32)
            sp = softplus2(z)
            if masked:
                sp = jnp.where(diag_valid, sp, 0.0)
            spb = sp.astype(bf16)
            tot_b = jnp.dot(spb[:, 2 * blk:], suffix_w, preferred_element_type=f32)
            if not first:
                tot_b = tot_b + car_ref[p, r0:, :]
            tot_a = jnp.dot(spb[:, :2 * blk], suffix_w, preferred_element_type=f32) + lane_bcast(tot_b)
            car_ref[p, r0:, :] = lane_bcast(tot_a)
            a = jnp.exp2(z - jnp.concatenate([tot_a, tot_b], axis=1))
            if masked:
                a = jnp.where(diag_valid, a, 0.0)
            av = jnp.dot(a.astype(bf16), stack(v_a, v_b), preferred_element_type=f32)
            if first:
                acc_ref[p, r0:, :] = av
                if r0 > 0:
                    acc_ref[p, :r0, :] = jnp.zeros((r0, V7X_LANES), f32)
                    car_ref[p, :r0, :] = jnp.zeros((r0, 2 * blk), f32)
            else:
                acc_ref[p, r0:, :] += av

    def q_tile(qi, carry):
        q0 = pl.multiple_of(qi * qt, qt)
        q_all = q_ref[pl.ds(q0, qt), :]
        q_tiles = [q_all[:, p * V7X_LANES:(p + 1) * V7X_LANES] for p in range(n_pairs)]
        for j in reversed(range(qt // SB_KT)):
            span(q_tiles, q0 + j * SB_KT, j)

        def k_span(i, c):
            span(q_tiles, pl.multiple_of(q0 - (i + 1) * SB_KT, SB_KT), None)
            return c

        lax.fori_loop(0, qi * (qt // SB_KT), k_span, 0)
        o_ref[pl.ds(q0, qt), :] = jnp.concatenate(
            [acc_ref[p] for p in range(n_pairs)], axis=1).astype(o_ref.dtype)
        return carry

    lax.fori_loop(0, seq // qt, q_tile, 0)


def _sb_attention(proj_sb, batch, seq):
    t = batch * seq
    n_pairs = SB_WIDTH // V7X_LANES
    est = 2 * 4 * seq * SB_WIDTH * 2 + n_pairs * SB_QT * 3 * V7X_LANES * 4 + 24 * 1024 * 1024
    return pl.pallas_call(
        functools.partial(_sb_kernel, seq=seq),
        grid=(batch,),
        in_specs=[pl.BlockSpec((seq, SB_WIDTH), lambda b: (b, 0)),
                  pl.BlockSpec((seq, SB_WIDTH), lambda b: (b, 1)),
                  pl.BlockSpec((seq, SB_WIDTH), lambda b: (b, 2))],
        out_specs=pl.BlockSpec((seq, SB_WIDTH), lambda b: (b, 0)),
        out_shape=jax.ShapeDtypeStruct((t, SB_WIDTH), bf16),
        scratch_shapes=[pltpu.VMEM((n_pairs, SB_QT, V7X_LANES), f32),
                        pltpu.VMEM((n_pairs, SB_QT, 2 * SB_BLOCK), f32)],
        compiler_params=_vmem_params(est, ("arbitrary",)),
        name="sb_attention",
    )(proj_sb, proj_sb, proj_sb)


def _route(sel, scores):
    c, s = sel, scores
    group_score = []
    for g in range(N_GROUPS):
        a, b, cc, d = c[4 * g:4 * g + 4]
        hi1, lo1 = jnp.maximum(a, b), jnp.minimum(a, b)
        hi2, lo2 = jnp.maximum(cc, d), jnp.minimum(cc, d)
        top1 = jnp.maximum(hi1, hi2)
        top2 = jnp.maximum(jnp.minimum(hi1, hi2), jnp.maximum(lo1, lo2))
        group_score.append(top1 + top2)
    best = group_score[0]
    gidx = jnp.zeros_like(best, dtype=i32)
    for g in range(1, N_GROUPS):
        better = group_score[g] > best
        gidx = jnp.where(better, g, gidx)
        best = jnp.where(better, group_score[g], best)
    vals, scs = [], []
    for j in range(EXPERTS_PER_GROUP):
        vj, sj = c[j], s[j]
        for g in range(1, N_GROUPS):
            pick = gidx == g
            vj = jnp.where(pick, c[4 * g + j], vj)
            sj = jnp.where(pick, s[4 * g + j], sj)
        vals.append(vj)
        scs.append(sj)
    m1, i1, w1 = vals[0], jnp.zeros_like(gidx), scs[0]
    for j in range(1, EXPERTS_PER_GROUP):
        better = vals[j] > m1
        m1 = jnp.where(better, vals[j], m1)
        i1 = jnp.where(better, j, i1)
        w1 = jnp.where(better, scs[j], w1)
    m2 = jnp.full_like(m1, -jnp.inf)
    i2 = jnp.full_like(i1, -1)
    w2 = jnp.zeros_like(w1)
    for j in range(EXPERTS_PER_GROUP):
        better = (i1 != j) & ((vals[j] > m2) | (i2 < 0))
        m2 = jnp.where(better, vals[j], m2)
        i2 = jnp.where(better, j, i2)
        w2 = jnp.where(better, scs[j], w2)
    denom = w1 + w2
    return w1 / denom, w2 / denom, gidx * EXPERTS_PER_GROUP + i1, gidx * EXPERTS_PER_GROUP + i2


def _mix_kernel(proj_ref, ysb_ref, x_ref, convw_ref, convb_ref, wg2_ref, bg_ref, normw_ref,
                wout_ref, lng_ref, lnb_ref, rwt_ref, rbt_ref,
                x1r_ref, rinfo_ref, counts_ref,
                halo_ref, state_ref, cnt_ref, *, ts, alpha):
    t_idx = pl.program_id(1)
    kp = GLA_KEY_PAD
    ck = GLA_CHUNK

    @pl.when((pl.program_id(0) == 0) & (t_idx == 0))
    def _():
        cnt_ref[...] = jnp.zeros_like(cnt_ref)

    @pl.when(t_idx == 0)
    def _():
        halo_ref[...] = jnp.zeros_like(halo_ref)
        state_ref[...] = jnp.zeros_like(state_ref)

    c_h = proj_ref[:, OFF_CONV:OFF_CONV + CONV_WIDTH].astype(f32)
    c_b = proj_ref[:, OFF_CONV + CONV_WIDTH:OFF_CONV + 2 * CONV_WIDTH].astype(f32)
    c_c = proj_ref[:, OFF_CONV + 2 * CONV_WIDTH:OFF_CONV + 3 * CONV_WIDTH].astype(f32)
    u = c_c * c_h
    ucat = jnp.concatenate([halo_ref[...], u], axis=0)
    halo_ref[...] = u[ts - V7X_SUBLANES:, :]
    u1 = ucat[V7X_SUBLANES - 1:V7X_SUBLANES - 1 + ts, :]
    u2 = ucat[V7X_SUBLANES - 2:V7X_SUBLANES - 2 + ts, :]
    y = convb_ref[...] + convw_ref[0:1, :] * u
    y = y + convw_ref[1:2, :] * u1
    y = y + convw_ref[2:3, :] * u2
    y_conv = c_b * y

    lr = proj_ref[:, OFF_LR:OFF_LR + V7X_LANES]
    pre = jnp.dot(lr, wg2_ref[...], preferred_element_type=f32) + bg_ref[...]
    log_a = (jnp.minimum(pre, 0.0) - jnp.log1p(jnp.exp(_neg_abs(pre)))) * (1.0 / GLA_GATE_NORM)
    chunk_shift = ck.bit_length() - 1
    dck = 2 * ck
    ir = lax.broadcasted_iota(i32, (dck, dck), 0)
    ic = lax.broadcasted_iota(i32, (dck, dck), 1)
    same_chunk = (ir >> chunk_shift) == (ic >> chunk_shift)
    intra_mask = same_chunk & (ic <= ir)
    cum_w = jnp.where(intra_mask, 1.0, 0.0).astype(bf16)
    tot_w = jnp.where(same_chunk, 1.0, 0.0).astype(bf16)
    cum_tot_w = jnp.concatenate([jnp.concatenate([cum_w, cum_w], axis=1),
                                 jnp.concatenate([tot_w, tot_w], axis=1)], axis=0)
    la_hi = log_a.astype(bf16)
    la_lo = (log_a - la_hi.astype(f32)).astype(bf16)
    bcum, blast = [], []
    for j in range(ts // dck):
        rows = slice(j * dck, (j + 1) * dck)
        r = jnp.dot(cum_tot_w, jnp.concatenate([la_hi[rows], la_lo[rows]], axis=0),
                    preferred_element_type=f32)
        bcum.append(r[:dck])
        blast.append(r[dck:])
    bcum = jnp.concatenate(bcum, axis=0)
    blast = jnp.concatenate(blast, axis=0)
    qf = proj_ref[:, OFF_GQ:OFF_GQ + kp].astype(f32)
    kf = proj_ref[:, OFF_GK:OFF_GK + kp].astype(f32)
    vb = proj_ref[:, OFF_GV:OFF_GV + GLA_WIDTH]
    q_in = ((qf * (GLA_DK ** -0.5)) * jnp.exp(bcum)).astype(bf16)
    k_in = (kf * jnp.exp(-bcum)).astype(bf16)
    k_te = (kf * jnp.exp(blast - bcum)).astype(bf16)
    decay = jnp.exp(blast)

    klane = lax.broadcasted_iota(i32, (dck, kp), 1)
    vlane = lax.broadcasted_iota(i32, (dck, GLA_WIDTH), 1)
    kmask = [(klane >= h * GLA_DK) & (klane < (h + 1) * GLA_DK) for h in range(GLA_HEADS)]
    vmask = [(vlane >= h * GLA_DV) & (vlane < (h + 1) * GLA_DV) for h in range(GLA_HEADS)]
    sr = lax.broadcasted_iota(i32, (GLA_HEADS * dck, dck), 0) & (dck - 1)
    sc_ = lax.broadcasted_iota(i32, (GLA_HEADS * dck, dck), 1)
    stacked_mask = ((sr >> chunk_shift) == (sc_ >> chunk_shift)) & (sc_ <= sr)
    o_intra = []
    for j in range(ts // dck):
        rows = slice(j * dck, (j + 1) * dck)
        q_j, k_j, v_j = q_in[rows], k_in[rows], vb[rows]
        q_st = jnp.concatenate([jnp.where(kmask[h], q_j, jnp.zeros_like(q_j)) for h in range(GLA_HEADS)],
                               axis=0)
        sc = lax.dot_general(q_st, k_j, (((1,), (1,)), ((), ())), preferred_element_type=f32)
        sc = jnp.where(stacked_mask, sc, 0.0).astype(bf16)
        oh = jnp.dot(sc, v_j, preferred_element_type=f32)
        o_j = oh[:dck]
        for h in range(1, GLA_HEADS):
            o_j = jnp.where(vmask[h], oh[h * dck:(h + 1) * dck], o_j)
        o_intra.append(o_j)
    o_intra = jnp.concatenate(o_intra, axis=0)

    srow = lax.broadcasted_iota(i32, (GLA_WIDTH, kp), 0)
    scol = lax.broadcasted_iota(i32, (GLA_WIDTH, kp), 1)
    blockdiag = jnp.zeros((GLA_WIDTH, kp), jnp.bool_)
    for h in range(GLA_HEADS):
        blockdiag = blockdiag | ((srow >= h * GLA_DV) & (srow < (h + 1) * GLA_DV)
                                 & (scol >= h * GLA_DK) & (scol < (h + 1) * GLA_DK))
    kvt = [lax.dot_general(vb[c * ck:(c + 1) * ck], k_te[c * ck:(c + 1) * ck], (((0,), (0,)), ((), ())),
                           preferred_element_type=f32) for c in range(ts // ck)]
    state = state_ref[...]
    o_inter = []
    for c in range(ts // ck):
        o_inter.append(lax.dot_general(q_in[c * ck:(c + 1) * ck], state.astype(bf16),
                                       (((1,), (1,)), ((), ())), preferred_element_type=f32))
        state = decay[c * ck:c * ck + 1, :] * state + jnp.where(blockdiag, kvt[c], 0.0)
    state_ref[...] = state
    o_all = o_intra + jnp.concatenate(o_inter, axis=0)

    o_sq = o_all * o_all
    vlane_t = lax.broadcasted_iota(i32, (ts, GLA_WIDTH), 1)
    inv = jnp.zeros((ts, GLA_WIDTH), f32)
    for h in range(GLA_HEADS):
        hm = (vlane_t >= h * GLA_DV) & (vlane_t < (h + 1) * GLA_DV)
        ms = jnp.sum(jnp.where(hm, o_sq, 0.0), axis=-1, keepdims=True) * (1.0 / GLA_DV)
        inv = jnp.where(hm, lax.rsqrt(ms + RMS_EPS), inv)
    gg = proj_ref[:, OFF_GG:OFF_GG + GLA_WIDTH].astype(f32)
    y_gla = (o_all * inv * normw_ref[...]) * (gg * (1.0 / (1.0 + jnp.exp(-gg))))

    mix = jnp.concatenate([y_conv.astype(bf16), ysb_ref[...], y_gla.astype(bf16)], axis=1)
    proj_out = jnp.dot(mix, wout_ref[...], preferred_element_type=f32)
    x1 = _layer_norm(alpha * x_ref[...] + proj_out, lng_ref[...], lnb_ref[...])
    _rows_to_tiles_store(x1r_ref, x1, ts)

    logits = lax.dot_general(rwt_ref[...], x1.astype(bf16), (((1,), (1,)), ((), ())),
                             preferred_element_type=f32)
    scores = 1.0 / (1.0 + jnp.exp(-logits))
    sel = scores + rbt_ref[...]
    g0, g1, e0, e1 = _route([sel[e:e + 1, :] for e in range(N_EXPERTS)],
                            [scores[e:e + 1, :] for e in range(N_EXPERTS)])
    eid = lax.broadcasted_iota(i32, (N_EXPERTS, ts), 0)
    pick0 = eid == e0
    pick1 = eid == e1
    onehot = jnp.where(pick0 | pick1, 1.0, 0.0)
    rr = lax.broadcasted_iota(i32, (ts, ts), 0)
    cc = lax.broadcasted_iota(i32, (ts, ts), 1)
    before = jnp.dot(onehot.astype(bf16), jnp.where(rr < cc, 1.0, 0.0).astype(bf16),
                     preferred_element_type=f32) + cnt_ref[...]
    rank0 = jnp.sum(jnp.where(pick0, before, 0.0), axis=0, keepdims=True)
    rank1 = jnp.sum(jnp.where(pick1, before, 0.0), axis=0, keepdims=True)
    new_cnt = cnt_ref[...] + jnp.sum(onehot, axis=1, keepdims=True)
    cnt_ref[...] = new_cnt
    counts_ref[...] = new_cnt[:, :V7X_LANES]
    zero = jnp.zeros_like(g0)
    rinfo_ref[...] = jnp.concatenate(
        [g0, g1, e0.astype(f32), e1.astype(f32), rank0, rank1, zero, zero], axis=0)


def _mix(proj, ysb, x2d, lw, batch, seq, alpha):
    t = batch * seq
    ts = MIX_TILE
    nt = seq // ts
    row = lambda b, s: (b * nt + s, 0)
    const = lambda b, s: (0, 0)
    est = (2 * (ts * R_WIDTH * 2 + ts * SB_WIDTH * 2 + 2 * ts * D_MODEL * 4 + D_MODEL * D_MODEL * 2)
           + 28 * 1024 * 1024)
    return pl.pallas_call(
        functools.partial(_mix_kernel, ts=ts, alpha=alpha),
        grid=(batch, nt),
        in_specs=[pl.BlockSpec((ts, R_WIDTH), row),
                  pl.BlockSpec((ts, SB_WIDTH), row),
                  pl.BlockSpec((ts, D_MODEL), row),
                  pl.BlockSpec((V7X_SUBLANES, CONV_WIDTH), const),
                  pl.BlockSpec((1, CONV_WIDTH), const),
                  pl.BlockSpec((V7X_LANES, GLA_KEY_PAD), const),
                  pl.BlockSpec((1, GLA_KEY_PAD), const),
                  pl.BlockSpec((1, GLA_WIDTH), const),
                  pl.BlockSpec((D_MODEL, D_MODEL), const),
                  pl.BlockSpec((1, D_MODEL), const),
                  pl.BlockSpec((1, D_MODEL), const),
                  pl.BlockSpec((N_EXPERTS, D_MODEL), const),
                  pl.BlockSpec((N_EXPERTS, ts), const)],
        out_specs=[pl.BlockSpec((ts * ROW_TILE, V7X_LANES), row),
                   pl.BlockSpec((INFO_ROWS, ts), lambda b, s: (0, b * nt + s)),
                   pl.BlockSpec((N_EXPERTS, V7X_LANES), const)],
        out_shape=[jax.ShapeDtypeStruct((t * ROW_TILE, V7X_LANES), f32),
                   jax.ShapeDtypeStruct((INFO_ROWS, t), f32),
                   jax.ShapeDtypeStruct((N_EXPERTS, V7X_LANES), f32)],
        scratch_shapes=[pltpu.VMEM((V7X_SUBLANES, CONV_WIDTH), f32),
                        pltpu.VMEM((GLA_WIDTH, GLA_KEY_PAD), f32),
                        pltpu.VMEM((N_EXPERTS, ts), f32)],
        compiler_params=_vmem_params(est, ("arbitrary", "arbitrary")),
        name="mix_ln_router",
    )(proj, ysb, x2d, lw["conv_w"], lw["conv_b"], lw["wg2"], lw["bg"], lw["normw"], lw["w_out"],
      lw["ln1_g"], lw["ln1_b"], lw["rwt"], lw["rbt"])


def _dest_plan(rinfo, counts, n_tok):
    tm = MOE_ROW_TILE
    n_rows = 2 * n_tok + N_EXPERTS * tm
    cnt = counts[:, 0].astype(i32)
    padded = ((cnt + tm - 1) // tm) * tm
    ends = jnp.cumsum(padded)
    starts = ends - padded
    ids = jnp.arange(N_EXPERTS, dtype=i32)[None, :]
    e0 = rinfo[2].astype(i32)
    e1 = rinfo[3].astype(i32)
    dest0 = rinfo[4].astype(i32) + jnp.sum(jnp.where(e0[:, None] == ids, starts[None, :], 0), axis=1)
    dest1 = rinfo[5].astype(i32) + jnp.sum(jnp.where(e1[:, None] == ids, starts[None, :], 0), axis=1)
    nb = n_tok // MOE_TOKEN_TILE
    dest = jnp.concatenate([dest0.reshape(nb, 1, MOE_TOKEN_TILE), dest1.reshape(nb, 1, MOE_TOKEN_TILE)], axis=2)
    tile_start = jnp.arange(n_rows // tm, dtype=i32) * tm
    tile_expert = jnp.minimum(jnp.sum((tile_start[:, None] >= ends[None, :]).astype(i32), axis=1),
                              N_EXPERTS - 1).astype(i32)
    tile_valid = (tile_start < ends[-1]).astype(i32)
    tile_first = (jnp.sum((tile_start[:, None] == starts[None, :]).astype(i32), axis=1) > 0).astype(i32) * tile_valid
    tail = ends[-1] + jnp.arange(N_EXPERTS, dtype=i32) * tm
    zstart = jnp.concatenate([jnp.where(padded > 0, ends - tm, -1),
                              jnp.where(tail < n_rows, tail, -1)]).astype(i32)
    return dest, zstart, tile_expert, tile_valid, tile_first, n_rows


def _dispatch_kernel(zstart_ref, dest_ref, x1r_ref, xs_hbm, zbuf, sem, zsem, *, tt):
    @pl.when(pl.program_id(0) == 0)
    def _():
        zbuf[...] = jnp.zeros_like(zbuf)

        def zero_copy(e):
            z0 = pl.multiple_of(jnp.maximum(zstart_ref[e], 0) * ROW_TILE, ROW_TILE)
            return pltpu.make_async_copy(zbuf, xs_hbm.at[pl.ds(z0, MOE_ROW_TILE * ROW_TILE), :], zsem)

        for e in range(2 * N_EXPERTS):
            @pl.when(zstart_ref[e] >= 0)
            def _():
                zero_copy(e).start()
        for e in range(2 * N_EXPERTS):
            @pl.when(zstart_ref[e] >= 0)
            def _():
                zero_copy(e).wait()

    def row_copy(r, j):
        d = dest_ref[0, 0, j]
        return pltpu.make_async_copy(
            x1r_ref.at[pl.ds(pl.multiple_of(r * ROW_TILE, ROW_TILE), ROW_TILE), :],
            xs_hbm.at[pl.ds(pl.multiple_of(d * ROW_TILE, ROW_TILE), ROW_TILE), :], sem)

    def issue(g, c):
        for u in range(DMA_UNROLL):
            r = g * DMA_UNROLL + u
            row_copy(r, r).start(priority=0)
            row_copy(r, tt + r).start(priority=1)
        return c

    lax.fori_loop(0, tt // DMA_UNROLL, issue, 0)
    whole = pltpu.make_async_copy(x1r_ref, xs_hbm.at[pl.ds(0, tt * ROW_TILE), :], sem)
    whole.wait()
    whole.wait()


def _dispatch(x1r, dest, zstart, n_tok, n_rows):
    tt = MOE_TOKEN_TILE
    est = 2 * tt * D_MODEL * 4 + MOE_ROW_TILE * D_MODEL * 4 + 8 * 1024 * 1024
    grid_spec = pltpu.PrefetchScalarGridSpec(
        num_scalar_prefetch=1,
        grid=(n_tok // tt,),
        in_specs=[pl.BlockSpec((1, 1, 2 * tt), lambda i, z: (i, 0, 0), memory_space=pltpu.SMEM),
                  pl.BlockSpec((tt * ROW_TILE, V7X_LANES), lambda i, z: (i, 0))],
        out_specs=pl.BlockSpec(memory_space=pl.ANY),
        scratch_shapes=[pltpu.VMEM((MOE_ROW_TILE * ROW_TILE, V7X_LANES), f32),
                        pltpu.SemaphoreType.DMA(()),
                        pltpu.SemaphoreType.DMA(())],
    )
    return pl.pallas_call(
        functools.partial(_dispatch_kernel, tt=tt),
        grid_spec=grid_spec,
        out_shape=jax.ShapeDtypeStruct((n_rows * ROW_TILE, V7X_LANES), f32),
        compiler_params=_vmem_params(est, ("arbitrary",)),
        name="dispatch",
    )(zstart, dest, x1r)


def _expert_kernel(texp_ref, tvalid_ref, tfirst_ref, xs_ref, w1_ref, w3_ref, w2_ref, ys_ref,
                   w1b, w3b, w2b, *, tm):
    i = pl.program_id(0)

    @pl.when(tfirst_ref[i] > 0)
    def _():
        w1b[...] = w1_ref[0].astype(bf16)
        w3b[...] = w3_ref[0].astype(bf16)
        w2b[...] = w2_ref[0].astype(bf16)

    @pl.when(tvalid_ref[i] > 0)
    def _():
        xb = _tiles_to_rows_load(xs_ref, tm).astype(bf16)
        a = jnp.dot(xb, w1b[...], preferred_element_type=f32)
        b = jnp.dot(xb, w3b[...], preferred_element_type=f32)
        h = (a * (1.0 / (1.0 + jnp.exp(-a)))) * b
        o = jnp.dot(h.astype(bf16), w2b[...], preferred_element_type=f32)
        _rows_to_tiles_store(ys_ref, o, tm)

    @pl.when(tvalid_ref[i] == 0)
    def _():
        ys_ref[...] = jnp.zeros_like(ys_ref)


def _experts(xs, tile_expert, tile_valid, tile_first, w1, w3, w2, layer, n_rows):
    tm = MOE_ROW_TILE
    base = layer * N_EXPERTS
    blk = (tm * ROW_TILE, V7X_LANES)
    est = 2 * (3 * D_MODEL * D_EXPERT * 4 + 2 * tm * D_MODEL * 4) + 3 * D_MODEL * D_EXPERT * 2 + 16 * 1024 * 1024
    grid_spec = pltpu.PrefetchScalarGridSpec(
        num_scalar_prefetch=3,
        grid=(n_rows // tm,),
        in_specs=[pl.BlockSpec(blk, lambda i, te, tv, tf: (i, 0)),
                  pl.BlockSpec((1, D_MODEL, D_EXPERT), lambda i, te, tv, tf: (base + te[i], 0, 0)),
                  pl.BlockSpec((1, D_MODEL, D_EXPERT), lambda i, te, tv, tf: (base + te[i], 0, 0)),
                  pl.BlockSpec((1, D_EXPERT, D_MODEL), lambda i, te, tv, tf: (base + te[i], 0, 0))],
        out_specs=pl.BlockSpec(blk, lambda i, te, tv, tf: (i, 0)),
        scratch_shapes=[pltpu.VMEM((D_MODEL, D_EXPERT), bf16),
                        pltpu.VMEM((D_MODEL, D_EXPERT), bf16),
                        pltpu.VMEM((D_EXPERT, D_MODEL), bf16)],
    )
    return pl.pallas_call(
        functools.partial(_expert_kernel, tm=tm),
        grid_spec=grid_spec,
        out_shape=jax.ShapeDtypeStruct((n_rows * ROW_TILE, V7X_LANES), f32),
        compiler_params=_vmem_params(est, ("arbitrary",)),
        name="experts",
    )(tile_expert, tile_valid, tile_first, xs, w1, w3, w2)


def _combine_kernel(dest_ref, dest_next_ref, ys_hbm, x1r_ref, rinfo_ref, g_ref, b_ref, o_ref,
                    buf, sem, *, tt, alpha):
    i = pl.program_id(0)
    n = pl.num_programs(0)
    slot = lax.rem(i, 2)
    nslot = 1 - slot

    def row_copy(d_ref, s, j):
        d = d_ref[0, 0, j]
        return pltpu.make_async_copy(
            ys_hbm.at[pl.ds(pl.multiple_of(d * ROW_TILE, ROW_TILE), ROW_TILE), :],
            buf.at[s, pl.ds(pl.multiple_of(j * ROW_TILE, ROW_TILE), ROW_TILE), :], sem.at[s])

    def wait_slot(s):
        pltpu.make_async_copy(ys_hbm.at[pl.ds(0, 2 * tt * ROW_TILE), :], buf.at[s], sem.at[s]).wait()

    @pl.when(i == 0)
    def _():
        def issue(g, c):
            for u in range(DMA_UNROLL):
                row_copy(dest_ref, 0, g * DMA_UNROLL + u).start(priority=u % 2)
            return c
        lax.fori_loop(0, 2 * tt // DMA_UNROLL, issue, 0)

    wait_slot(slot)
    gates = jnp.transpose(rinfo_ref[...])
    tc = tt // COMBINE_CHUNKS
    per_chunk = 2 * tt // COMBINE_CHUNKS
    for c in range(COMBINE_CHUNKS):
        for j in range(c * per_chunk, (c + 1) * per_chunk):
            row_copy(dest_next_ref, nslot, j).start(priority=j % 2)
        r0 = c * tc * ROW_TILE
        o0 = _tiles_to_rows_load(buf.at[slot, pl.ds(r0, tc * ROW_TILE), :], tc)
        o1 = _tiles_to_rows_load(buf.at[slot, pl.ds(tt * ROW_TILE + r0, tc * ROW_TILE), :], tc)
        x1 = _tiles_to_rows_load(x1r_ref.at[pl.ds(r0, tc * ROW_TILE), :], tc)
        g = gates[c * tc:(c + 1) * tc]
        ffn = g[:, 0:1] * o0 + g[:, 1:2] * o1
        o_ref[c * tc:(c + 1) * tc, :] = _layer_norm(alpha * x1 + ffn, g_ref[...], b_ref[...])

    @pl.when(i == n - 1)
    def _():
        wait_slot(nslot)


def _combine(ys, dest, x1r, rinfo, ln_g, ln_b, n_tok, alpha):
    tt = MOE_TOKEN_TILE
    nb = n_tok // tt
    est = 2 * (tt * D_MODEL * 4 * 2 + tt * V7X_LANES * 4) + 2 * 2 * tt * D_MODEL * 4 + 24 * 1024 * 1024
    return pl.pallas_call(
        functools.partial(_combine_kernel, tt=tt, alpha=alpha),
        grid=(nb,),
        in_specs=[pl.BlockSpec((1, 1, 2 * tt), lambda i: (i, 0, 0), memory_space=pltpu.SMEM),
                  pl.BlockSpec((1, 1, 2 * tt), lambda i: (jnp.minimum(i + 1, nb - 1), 0, 0),
                               memory_space=pltpu.SMEM),
                  pl.BlockSpec(memory_space=pl.ANY),
                  pl.BlockSpec((tt * ROW_TILE, V7X_LANES), lambda i: (i, 0)),
                  pl.BlockSpec((INFO_ROWS, tt), lambda i: (0, i)),
                  pl.BlockSpec((1, D_MODEL), lambda i: (0, 0)),
                  pl.BlockSpec((1, D_MODEL), lambda i: (0, 0))],
        out_specs=pl.BlockSpec((tt, D_MODEL), lambda i: (i, 0)),
        out_shape=jax.ShapeDtypeStruct((n_tok, D_MODEL), f32),
        scratch_shapes=[pltpu.VMEM((2, 2 * tt * ROW_TILE, V7X_LANES), f32),
                        pltpu.SemaphoreType.DMA((2,))],
        compiler_params=_vmem_params(est, ("arbitrary",)),
        name="combine_ln",
    )(dest, dest, ys, x1r, rinfo, ln_g, ln_b)


def _prep_w_in(w_in_l):
    o = 3 * CONV_WIDTH
    conv = w_in_l[:, :o]
    sbq = w_in_l[:, o:o + SB_WIDTH] * (SB_HEAD_DIM ** -0.5 * LOG2E)
    sbk = w_in_l[:, o + SB_WIDTH:o + 2 * SB_WIDTH]
    sbv = w_in_l[:, o + 2 * SB_WIDTH:o + 3 * SB_WIDTH]
    o = o + 3 * SB_WIDTH
    gq = w_in_l[:, o:o + GLA_KEY_WIDTH]
    gk = w_in_l[:, o + GLA_KEY_WIDTH:o + 2 * GLA_KEY_WIDTH]
    o = o + 2 * GLA_KEY_WIDTH
    gv = w_in_l[:, o:o + GLA_WIDTH]
    gg = w_in_l[:, o + GLA_WIDTH:o + 2 * GLA_WIDTH]
    lr = w_in_l[:, o + 2 * GLA_WIDTH:]
    zk = jnp.zeros((D_MODEL, GLA_KEY_PAD - GLA_KEY_WIDTH), f32)
    zl = jnp.zeros((D_MODEL, V7X_LANES - GLA_GATE_RANK), f32)
    return jnp.concatenate([sbq, sbk, sbv, conv, gq, zk, gk, zk, gv, gg, lr, zl], axis=1).astype(bf16)


def kernel(x, w_in, conv_w, conv_b, gla_w_gate2, gla_b_gate, gla_norm_w, w_out, ln1_g, ln1_b,
           router_w, router_bias, w1, w3, w2, ln2_g, ln2_b):
    batch, seq, d = x.shape
    depth = w_in.shape[0]
    n_tok = batch * seq
    alpha = (2 * depth) ** 0.25

    w1f = w1.reshape(depth * N_EXPERTS, D_MODEL, D_EXPERT)
    w3f = w3.reshape(depth * N_EXPERTS, D_MODEL, D_EXPERT)
    w2f = w2.reshape(depth * N_EXPERTS, D_EXPERT, D_MODEL)
    rwt = router_w.T.astype(bf16)
    rbt = jnp.broadcast_to(router_bias.reshape(N_EXPERTS, 1), (N_EXPERTS, MIX_TILE))

    x2d = x.reshape(n_tok, d)
    for l in range(depth):
        lw = dict(
            conv_w=jnp.pad(conv_w[l], ((0, V7X_SUBLANES - CONV_K), (0, 0))),
            conv_b=conv_b[l].reshape(1, CONV_WIDTH),
            wg2=jnp.pad(gla_w_gate2[l], ((0, V7X_LANES - GLA_GATE_RANK),
                                         (0, GLA_KEY_PAD - GLA_KEY_WIDTH))).astype(bf16),
            bg=jnp.pad(gla_b_gate[l].reshape(1, GLA_KEY_WIDTH), ((0, 0), (0, GLA_KEY_PAD - GLA_KEY_WIDTH))),
            normw=jnp.tile(gla_norm_w[l], GLA_HEADS).reshape(1, GLA_WIDTH),
            w_out=w_out[l].astype(bf16),
            ln1_g=ln1_g[l].reshape(1, d), ln1_b=ln1_b[l].reshape(1, d),
            rwt=rwt, rbt=rbt)
        proj_sb, proj_rest = _in_proj(x2d, _prep_w_in(w_in[l]))
        ysb = _sb_attention(proj_sb, batch, seq)
        x1r, rinfo, counts = _mix(proj_rest, ysb, x2d, lw, batch, seq, alpha)
        dest, zstart, tile_expert, tile_valid, tile_first, n_rows = _dest_plan(rinfo, counts, n_tok)
        xs = _dispatch(x1r, dest, zstart, n_tok, n_rows)
        ys = _experts(xs, tile_expert, tile_valid, tile_first, w1f, w3f, w2f, l, n_rows)
        x2d = _combine(ys, dest, x1r, rinfo, ln2_g[l].reshape(1, d), ln2_b[l].reshape(1, d), n_tok, alpha)
    return x2d.reshape(batch, seq, d)
```

```python
import functools

import jax
import jax.numpy as jnp
from jax import lax
from jax.experimental import pallas as pl
from jax.experimental.pallas import tpu as pltpu

f32 = jnp.float32
bf16 = jnp.bfloat16
i32 = jnp.int32

V7X_LANES = 128
V7X_SUBLANES = 8
V7X_VMEM_BYTES = 64 * 1024 * 1024

D_MODEL = 1024
CONV_WIDTH = 256
CONV_K = 3
SB_HEADS = 6
SB_HEAD_DIM = 64
SB_WIDTH = SB_HEADS * SB_HEAD_DIM
SB_BLOCK = 128
SB_QT = 512
SB_KT = 2 * SB_BLOCK
SB_SEQS = 2
GLA_HEADS = 4
GLA_DV = 96
GLA_DK = 48
GLA_WIDTH = GLA_HEADS * GLA_DV
GLA_KEY_WIDTH = GLA_HEADS * GLA_DK
GLA_KEY_PAD = 256
GLA_GATE_RANK = 16
GLA_GATE_NORM = 16.0
GLA_CHUNK = 64
N_EXPERTS = 16
N_GROUPS = 4
EXPERTS_PER_GROUP = 4
D_EXPERT = 512
LN_EPS = 1e-5
RMS_EPS = 1e-6
LOG2E = 1.4426950408889634

SB_PROJ_WIDTH = 3 * SB_WIDTH
OFF_CONV = 0
OFF_GQ = 3 * CONV_WIDTH
OFF_GK = OFF_GQ + GLA_KEY_PAD
OFF_GV = OFF_GK + GLA_KEY_PAD
OFF_GG = OFF_GV + GLA_WIDTH
OFF_LR = OFF_GG + GLA_WIDTH
R_WIDTH = OFF_LR + V7X_LANES
P_WIDTH = SB_PROJ_WIDTH + R_WIDTH

ROW_TILE = V7X_SUBLANES
MIX_TILE = 1024
MOE_TOKEN_TILE = 512
MOE_ROW_TILE = 512
DMA_UNROLL = 8
COMBINE_CHUNKS = 8
INFO_ROWS = V7X_SUBLANES


def _vmem_params(est_bytes, semantics):
    limit = int(min(V7X_VMEM_BYTES - 6 * 1024 * 1024, max(est_bytes, 16 * 1024 * 1024)))
    return pltpu.CompilerParams(dimension_semantics=semantics, vmem_limit_bytes=limit)


def _neg_abs(x):
    bits = lax.bitcast_convert_type(x, jnp.uint32) | jnp.uint32(0x80000000)
    return lax.bitcast_convert_type(bits, f32)


def _rows_to_tiles_store(ref, val, n_rows):
    for j in range(D_MODEL // V7X_LANES):
        ref[pl.ds(j, n_rows, stride=ROW_TILE), :] = val[:, j * V7X_LANES:(j + 1) * V7X_LANES]


def _tiles_to_rows_load(ref, n_rows):
    return jnp.concatenate(
        [ref[pl.ds(j, n_rows, stride=ROW_TILE), :] for j in range(D_MODEL // V7X_LANES)], axis=1)


def _layer_norm(h, g, b):
    mu = jnp.mean(h, axis=-1, keepdims=True)
    c = h - mu
    var = jnp.mean(c * c, axis=-1, keepdims=True)
    return c * lax.rsqrt(var + LN_EPS) * g + b


def _in_proj_kernel(x_ref, w_ref, osb_ref, orest_ref):
    xb = x_ref[...].astype(bf16)
    n_chunk = 256
    for n0 in range(0, P_WIDTH, n_chunk):
        r = jnp.dot(xb, w_ref[:, n0:n0 + n_chunk], preferred_element_type=f32).astype(bf16)
        lo, hi = max(n0, 0), min(n0 + n_chunk, SB_PROJ_WIDTH)
        if hi > lo:
            osb_ref[:, lo:hi] = r[:, lo - n0:hi - n0]
        lo, hi = max(n0, SB_PROJ_WIDTH), n0 + n_chunk
        if hi > lo:
            orest_ref[:, lo - SB_PROJ_WIDTH:hi - SB_PROJ_WIDTH] = r[:, lo - n0:hi - n0]


def _in_proj(x2d, w):
    t = x2d.shape[0]
    tm = 512
    est = 2 * (tm * D_MODEL * 4 + D_MODEL * P_WIDTH * 2 + tm * P_WIDTH * 2) + 8 * 1024 * 1024
    return pl.pallas_call(
        _in_proj_kernel,
        grid=(t // tm,),
        in_specs=[pl.BlockSpec((tm, D_MODEL), lambda i: (i, 0)),
                  pl.BlockSpec((D_MODEL, P_WIDTH), lambda i: (0, 0))],
        out_specs=[pl.BlockSpec((tm, SB_PROJ_WIDTH), lambda i: (i, 0)),
                   pl.BlockSpec((tm, R_WIDTH), lambda i: (i, 0))],
        out_shape=[jax.ShapeDtypeStruct((t, SB_PROJ_WIDTH), bf16),
                   jax.ShapeDtypeStruct((t, R_WIDTH), bf16)],
        compiler_params=_vmem_params(est, ("arbitrary",)),
        name="in_proj",
    )(x2d, w)


def _sb_kernel(q_ref, k_ref, v_ref, o_ref, acc_ref, car_ref, *, seq):
    blk = SB_BLOCK
    qt = SB_QT
    n_pairs = SB_WIDTH // V7X_LANES
    klane = lax.broadcasted_iota(i32, (blk, V7X_LANES), 1)
    head_mask = [klane < SB_HEAD_DIM, klane >= SB_HEAD_DIM]
    wr = lax.broadcasted_iota(i32, (2 * blk, 2 * blk), 0)
    wc = lax.broadcasted_iota(i32, (2 * blk, 2 * blk), 1)
    suffix_w = jnp.where(((wr >= blk) == (wc >= blk)) & (wr >= wc), 1.0, 0.0).astype(bf16)
    def softplus2(z):
        return jnp.maximum(z, 0.0) + jnp.log2(1.0 + jnp.exp2(_neg_abs(z)))

    def lane_bcast(tot):
        n = tot.shape[0]
        return jnp.concatenate([jnp.broadcast_to(tot[:, 0:1], (n, blk)),
                                jnp.broadcast_to(tot[:, blk:blk + 1], (n, blk))], axis=1)

    def span(q_tiles, k0, diag_j):
        masked = diag_j is not None
        first = masked and diag_j == qt // SB_KT - 1
        r0 = diag_j * SB_KT if masked else 0
        nr = qt - r0
        if masked:
            ri = lax.broadcasted_iota(i32, (nr, 4 * blk), 0)
            ci = lax.broadcasted_iota(i32, (nr, 4 * blk), 1)
            diag_valid = (ci & (blk - 1)) + jnp.where(ci >= 2 * blk, blk, 0) < ri
        for sq in range(SB_SEQS):
            base = sq * seq
            k_a = k_ref[pl.ds(base + k0, blk), :]
            k_b = k_ref[pl.ds(base + k0 + blk, blk), :]
            v_a = v_ref[pl.ds(base + k0, blk), :]
            v_b = v_ref[pl.ds(base + k0 + blk, blk), :]
            for p in range(n_pairs):
                sl = slice(p * V7X_LANES, (p + 1) * V7X_LANES)
                u = sq * n_pairs + p

                def stack(x_a, x_b):
                    zero = jnp.zeros((blk, V7X_LANES), bf16)
                    return jnp.concatenate([jnp.where(head_mask[0], x_a[:, sl], zero),
                                            jnp.where(head_mask[1], x_a[:, sl], zero),
                                            jnp.where(head_mask[0], x_b[:, sl], zero),
                                            jnp.where(head_mask[1], x_b[:, sl], zero)], axis=0)

                z = lax.dot_general(q_tiles[u][r0:], stack(k_a, k_b), (((1,), (1,)), ((), ())),
                                    preferred_element_type=f32)
                sp = softplus2(z)
                if masked:
                    sp = jnp.where(diag_valid, sp, 0.0)
                spb = sp.astype(bf16)
                tot_b = jnp.dot(spb[:, 2 * blk:], suffix_w, preferred_element_type=f32)
                if not first:
                    tot_b = tot_b + car_ref[u, r0:, :]
                tot_a = jnp.dot(spb[:, :2 * blk], suffix_w, preferred_element_type=f32) + lane_bcast(tot_b)
                car_ref[u, r0:, :] = lane_bcast(tot_a)
                a = jnp.exp2(z - jnp.concatenate([tot_a, tot_b], axis=1))
                if masked:
                    a = jnp.where(diag_valid, a, 0.0)
                av = jnp.dot(a.astype(bf16), stack(v_a, v_b), preferred_element_type=f32)
                if first:
                    acc_ref[u, r0:, :] = av
                    if r0 > 0:
                        acc_ref[u, :r0, :] = jnp.zeros((r0, V7X_LANES), f32)
                        car_ref[u, :r0, :] = jnp.zeros((r0, 2 * blk), f32)
                else:
                    acc_ref[u, r0:, :] += av

    def q_tile(qi, carry):
        q0 = pl.multiple_of(qi * qt, qt)
        q_tiles = []
        for sq in range(SB_SEQS):
            q_all = q_ref[pl.ds(sq * seq + q0, qt), :]
            q_tiles += [q_all[:, p * V7X_LANES:(p + 1) * V7X_LANES] for p in range(n_pairs)]
        for j in reversed(range(qt // SB_KT)):
            span(q_tiles, q0 + j * SB_KT, j)

        def k_span(i, c):
            span(q_tiles, pl.multiple_of(q0 - (i + 1) * SB_KT, SB_KT), None)
            return c

        lax.fori_loop(0, qi * (qt // SB_KT), k_span, 0)
        for sq in range(SB_SEQS):
            o_ref[pl.ds(sq * seq + q0, qt), :] = jnp.concatenate(
                [acc_ref[sq * n_pairs + p] for p in range(n_pairs)], axis=1).astype(o_ref.dtype)
        return carry

    lax.fori_loop(0, seq // qt, q_tile, 0)


def _sb_attention(proj_sb, batch, seq):
    t = batch * seq
    n_pairs = SB_WIDTH // V7X_LANES
    rows = SB_SEQS * seq
    est = 2 * 4 * rows * SB_WIDTH * 2 + SB_SEQS * n_pairs * SB_QT * 3 * V7X_LANES * 4 + 24 * 1024 * 1024
    return pl.pallas_call(
        functools.partial(_sb_kernel, seq=seq),
        grid=(batch // SB_SEQS,),
        in_specs=[pl.BlockSpec((rows, SB_WIDTH), lambda b: (b, 0)),
                  pl.BlockSpec((rows, SB_WIDTH), lambda b: (b, 1)),
                  pl.BlockSpec((rows, SB_WIDTH), lambda b: (b, 2))],
        out_specs=pl.BlockSpec((rows, SB_WIDTH), lambda b: (b, 0)),
        out_shape=jax.ShapeDtypeStruct((t, SB_WIDTH), bf16),
        scratch_shapes=[pltpu.VMEM((SB_SEQS * n_pairs, SB_QT, V7X_LANES), f32),
                        pltpu.VMEM((SB_SEQS * n_pairs, SB_QT, 2 * SB_BLOCK), f32)],
        compiler_params=_vmem_params(est, ("arbitrary",)),
        name="sb_attention",
    )(proj_sb, proj_sb, proj_sb)


def _route(sel, scores):
    c, s = sel, scores
    group_score = []
    for g in range(N_GROUPS):
        a, b, cc, d = c[4 * g:4 * g + 4]
        hi1, lo1 = jnp.maximum(a, b), jnp.minimum(a, b)
        hi2, lo2 = jnp.maximum(cc, d), jnp.minimum(cc, d)
        top1 = jnp.maximum(hi1, hi2)
        top2 = jnp.maximum(jnp.minimum(hi1, hi2), jnp.maximum(lo1, lo2))
        group_score.append(top1 + top2)
    best = group_score[0]
    gidx = jnp.zeros_like(best, dtype=i32)
    for g in range(1, N_GROUPS):
        better = group_score[g] > best
        gidx = jnp.where(better, g, gidx)
        best = jnp.where(better, group_score[g], best)
    vals, scs = [], []
    for j in range(EXPERTS_PER_GROUP):
        vj, sj = c[j], s[j]
        for g in range(1, N_GROUPS):
            pick = gidx == g
            vj = jnp.where(pick, c[4 * g + j], vj)
            sj = jnp.where(pick, s[4 * g + j], sj)
        vals.append(vj)
        scs.append(sj)
    m1, i1, w1 = vals[0], jnp.zeros_like(gidx), scs[0]
    for j in range(1, EXPERTS_PER_GROUP):
        better = vals[j] > m1
        m1 = jnp.where(better, vals[j], m1)
        i1 = jnp.where(better, j, i1)
        w1 = jnp.where(better, scs[j], w1)
    m2 = jnp.full_like(m1, -jnp.inf)
    i2 = jnp.full_like(i1, -1)
    w2 = jnp.zeros_like(w1)
    for j in range(EXPERTS_PER_GROUP):
        better = (i1 != j) & ((vals[j] > m2) | (i2 < 0))
        m2 = jnp.where(better, vals[j], m2)
        i2 = jnp.where(better, j, i2)
        w2 = jnp.where(better, scs[j], w2)
    denom = w1 + w2
    return w1 / denom, w2 / denom, gidx * EXPERTS_PER_GROUP + i1, gidx * EXPERTS_PER_GROUP + i2


def _mix_kernel(proj_ref, ysb_ref, x_ref, convw_ref, convb_ref, wg2_ref, bg_ref, normw_ref,
                wout_ref, lng_ref, lnb_ref, rwt_ref, rbt_ref,
                x1r_ref, rinfo_ref, counts_ref,
                halo_ref, state_ref, cnt_ref, *, ts, alpha):
    t_idx = pl.program_id(1)
    kp = GLA_KEY_PAD
    ck = GLA_CHUNK

    @pl.when((pl.program_id(0) == 0) & (t_idx == 0))
    def _():
        cnt_ref[...] = jnp.zeros_like(cnt_ref)

    @pl.when(t_idx == 0)
    def _():
        halo_ref[...] = jnp.zeros_like(halo_ref)
        state_ref[...] = jnp.zeros_like(state_ref)

    c_h = proj_ref[:, OFF_CONV:OFF_CONV + CONV_WIDTH].astype(f32)
    c_b = proj_ref[:, OFF_CONV + CONV_WIDTH:OFF_CONV + 2 * CONV_WIDTH].astype(f32)
    c_c = proj_ref[:, OFF_CONV + 2 * CONV_WIDTH:OFF_CONV + 3 * CONV_WIDTH].astype(f32)
    u = c_c * c_h
    ucat = jnp.concatenate([halo_ref[...], u], axis=0)
    halo_ref[...] = u[ts - V7X_SUBLANES:, :]
    u1 = ucat[V7X_SUBLANES - 1:V7X_SUBLANES - 1 + ts, :]
    u2 = ucat[V7X_SUBLANES - 2:V7X_SUBLANES - 2 + ts, :]
    y = convb_ref[...] + convw_ref[0:1, :] * u
    y = y + convw_ref[1:2, :] * u1
    y = y + convw_ref[2:3, :] * u2
    y_conv = c_b * y

    lr = proj_ref[:, OFF_LR:OFF_LR + V7X_LANES]
    pre = jnp.dot(lr, wg2_ref[...], preferred_element_type=f32) + bg_ref[...]
    log_a = (jnp.minimum(pre, 0.0) - jnp.log1p(jnp.exp(_neg_abs(pre)))) * (1.0 / GLA_GATE_NORM)
    chunk_shift = ck.bit_length() - 1
    dck = 2 * ck
    ir = lax.broadcasted_iota(i32, (dck, dck), 0)
    ic = lax.broadcasted_iota(i32, (dck, dck), 1)
    same_chunk = (ir >> chunk_shift) == (ic >> chunk_shift)
    intra_mask = same_chunk & (ic <= ir)
    cum_w = jnp.where(intra_mask, 1.0, 0.0).astype(bf16)
    tot_w = jnp.where(same_chunk, 1.0, 0.0).astype(bf16)
    cum_tot_w = jnp.concatenate([jnp.concatenate([cum_w, cum_w], axis=1),
                                 jnp.concatenate([tot_w, tot_w], axis=1)], axis=0)
    la_hi = log_a.astype(bf16)
    la_lo = (log_a - la_hi.astype(f32)).astype(bf16)
    bcum, blast = [], []
    for j in range(ts // dck):
        rows = slice(j * dck, (j + 1) * dck)
        r = jnp.dot(cum_tot_w, jnp.concatenate([la_hi[rows], la_lo[rows]], axis=0),
                    preferred_element_type=f32)
        bcum.append(r[:dck])
        blast.append(r[dck:])
    bcum = jnp.concatenate(bcum, axis=0)
    blast = jnp.concatenate(blast, axis=0)
    qf = proj_ref[:, OFF_GQ:OFF_GQ + kp].astype(f32)
    kf = proj_ref[:, OFF_GK:OFF_GK + kp].astype(f32)
    vb = proj_ref[:, OFF_GV:OFF_GV + GLA_WIDTH]
    q_in = ((qf * (GLA_DK ** -0.5)) * jnp.exp(bcum)).astype(bf16)
    k_in = (kf * jnp.exp(-bcum)).astype(bf16)
    k_te = (kf * jnp.exp(blast - bcum)).astype(bf16)
    decay = jnp.exp(blast)

    klane = lax.broadcasted_iota(i32, (dck, kp), 1)
    vlane = lax.broadcasted_iota(i32, (dck, GLA_WIDTH), 1)
    kmask = [(klane >= h * GLA_DK) & (klane < (h + 1) * GLA_DK) for h in range(GLA_HEADS)]
    vmask = [(vlane >= h * GLA_DV) & (vlane < (h + 1) * GLA_DV) for h in range(GLA_HEADS)]
    sr = lax.broadcasted_iota(i32, (GLA_HEADS * dck, dck), 0) & (dck - 1)
    sc_ = lax.broadcasted_iota(i32, (GLA_HEADS * dck, dck), 1)
    stacked_mask = ((sr >> chunk_shift) == (sc_ >> chunk_shift)) & (sc_ <= sr)
    o_intra = []
    for j in range(ts // dck):
        rows = slice(j * dck, (j + 1) * dck)
        q_j, k_j, v_j = q_in[rows], k_in[rows], vb[rows]
        q_st = jnp.concatenate([jnp.where(kmask[h], q_j, jnp.zeros_like(q_j)) for h in range(GLA_HEADS)],
                               axis=0)
        sc = lax.dot_general(q_st, k_j, (((1,), (1,)), ((), ())), preferred_element_type=f32)
        sc = jnp.where(stacked_mask, sc, 0.0).astype(bf16)
        oh = jnp.dot(sc, v_j, preferred_element_type=f32)
        o_j = oh[:dck]
        for h in range(1, GLA_HEADS):
            o_j = jnp.where(vmask[h], oh[h * dck:(h + 1) * dck], o_j)
        o_intra.append(o_j)
    o_intra = jnp.concatenate(o_intra, axis=0)

    srow = lax.broadcasted_iota(i32, (GLA_WIDTH, kp), 0)
    scol = lax.broadcasted_iota(i32, (GLA_WIDTH, kp), 1)
    blockdiag = jnp.zeros((GLA_WIDTH, kp), jnp.bool_)
    for h in range(GLA_HEADS):
        blockdiag = blockdiag | ((srow >= h * GLA_DV) & (srow < (h + 1) * GLA_DV)
                                 & (scol >= h * GLA_DK) & (scol < (h + 1) * GLA_DK))
    kvt = [lax.dot_general(vb[c * ck:(c + 1) * ck], k_te[c * ck:(c + 1) * ck], (((0,), (0,)), ((), ())),
                           preferred_element_type=f32) for c in range(ts // ck)]
    state = state_ref[...]
    o_inter = []
    for c in range(ts // ck):
        o_inter.append(lax.dot_general(q_in[c * ck:(c + 1) * ck], state.astype(bf16),
                                       (((1,), (1,)), ((), ())), preferred_element_type=f32))
        state = decay[c * ck:c * ck + 1, :] * state + jnp.where(blockdiag, kvt[c], 0.0)
    state_ref[...] = state
    o_all = o_intra + jnp.concatenate(o_inter, axis=0)

    o_sq = o_all * o_all
    vlane_t = lax.broadcasted_iota(i32, (ts, GLA_WIDTH), 1)
    inv = jnp.zeros((ts, GLA_WIDTH), f32)
    for h in range(GLA_HEADS):
        hm = (vlane_t >= h * GLA_DV) & (vlane_t < (h + 1) * GLA_DV)
        ms = jnp.sum(jnp.where(hm, o_sq, 0.0), axis=-1, keepdims=True) * (1.0 / GLA_DV)
        inv = jnp.where(hm, lax.rsqrt(ms + RMS_EPS), inv)
    gg = proj_ref[:, OFF_GG:OFF_GG + GLA_WIDTH].astype(f32)
    y_gla = (o_all * inv * normw_ref[...]) * (gg * (1.0 / (1.0 + jnp.exp(-gg))))

    mix = jnp.concatenate([y_conv.astype(bf16), ysb_ref[...], y_gla.astype(bf16)], axis=1)
    proj_out = jnp.dot(mix, wout_ref[...], preferred_element_type=f32)
    x1 = _layer_norm(alpha * x_ref[...] + proj_out, lng_ref[...], lnb_ref[...])
    _rows_to_tiles_store(x1r_ref, x1, ts)

    logits = lax.dot_general(rwt_ref[...], x1.astype(bf16), (((1,), (1,)), ((), ())),
                             preferred_element_type=f32)
    scores = 1.0 / (1.0 + jnp.exp(-logits))
    sel = scores + rbt_ref[...]
    g0, g1, e0, e1 = _route([sel[e:e + 1, :] for e in range(N_EXPERTS)],
                            [scores[e:e + 1, :] for e in range(N_EXPERTS)])
    eid = lax.broadcasted_iota(i32, (N_EXPERTS, ts), 0)
    pick0 = eid == e0
    pick1 = eid == e1
    onehot = jnp.where(pick0 | pick1, 1.0, 0.0)
    rr = lax.broadcasted_iota(i32, (ts, ts), 0)
    cc = lax.broadcasted_iota(i32, (ts, ts), 1)
    before = jnp.dot(onehot.astype(bf16), jnp.where(rr < cc, 1.0, 0.0).astype(bf16),
                     preferred_element_type=f32) + cnt_ref[...]
    rank0 = jnp.sum(jnp.where(pick0, before, 0.0), axis=0, keepdims=True)
    rank1 = jnp.sum(jnp.where(pick1, before, 0.0), axis=0, keepdims=True)
    new_cnt = cnt_ref[...] + jnp.sum(onehot, axis=1, keepdims=True)
    cnt_ref[...] = new_cnt
    counts_ref[...] = new_cnt[:, :V7X_LANES]
    zero = jnp.zeros_like(g0)
    rinfo_ref[...] = jnp.concatenate(
        [g0, g1, e0.astype(f32), e1.astype(f32), rank0, rank1, zero, zero], axis=0)


def _mix(proj, ysb, x2d, lw, batch, seq, alpha):
    t = batch * seq
    ts = MIX_TILE
    nt = seq // ts
    row = lambda b, s: (b * nt + s, 0)
    const = lambda b, s: (0, 0)
    est = (2 * (ts * R_WIDTH * 2 + ts * SB_WIDTH * 2 + 2 * ts * D_MODEL * 4 + D_MODEL * D_MODEL * 2)
           + 28 * 1024 * 1024)
    return pl.pallas_call(
        functools.partial(_mix_kernel, ts=ts, alpha=alpha),
        grid=(batch, nt),
        in_specs=[pl.BlockSpec((ts, R_WIDTH), row),
                  pl.BlockSpec((ts, SB_WIDTH), row),
                  pl.BlockSpec((ts, D_MODEL), row),
                  pl.BlockSpec((V7X_SUBLANES, CONV_WIDTH), const),
                  pl.BlockSpec((1, CONV_WIDTH), const),
                  pl.BlockSpec((V7X_LANES, GLA_KEY_PAD), const),
                  pl.BlockSpec((1, GLA_KEY_PAD), const),
                  pl.BlockSpec((1, GLA_WIDTH), const),
                  pl.BlockSpec((D_MODEL, D_MODEL), const),
                  pl.BlockSpec((1, D_MODEL), const),
                  pl.BlockSpec((1, D_MODEL), const),
                  pl.BlockSpec((N_EXPERTS, D_MODEL), const),
                  pl.BlockSpec((N_EXPERTS, ts), const)],
        out_specs=[pl.BlockSpec((ts * ROW_TILE, V7X_LANES), row),
                   pl.BlockSpec((INFO_ROWS, ts), lambda b, s: (0, b * nt + s)),
                   pl.BlockSpec((N_EXPERTS, V7X_LANES), const)],
        out_shape=[jax.ShapeDtypeStruct((t * ROW_TILE, V7X_LANES), f32),
                   jax.ShapeDtypeStruct((INFO_ROWS, t), f32),
                   jax.ShapeDtypeStruct((N_EXPERTS, V7X_LANES), f32)],
        scratch_shapes=[pltpu.VMEM((V7X_SUBLANES, CONV_WIDTH), f32),
                        pltpu.VMEM((GLA_WIDTH, GLA_KEY_PAD), f32),
                        pltpu.VMEM((N_EXPERTS, ts), f32)],
        compiler_params=_vmem_params(est, ("arbitrary", "arbitrary")),
        name="mix_ln_router",
    )(proj, ysb, x2d, lw["conv_w"], lw["conv_b"], lw["wg2"], lw["bg"], lw["normw"], lw["w_out"],
      lw["ln1_g"], lw["ln1_b"], lw["rwt"], lw["rbt"])


def _dest_plan(rinfo, counts, n_tok):
    tm = MOE_ROW_TILE
    n_rows = 2 * n_tok + N_EXPERTS * tm
    cnt = counts[:, 0].astype(i32)
    padded = ((cnt + tm - 1) // tm) * tm
    ends = jnp.cumsum(padded)
    starts = ends - padded
    ids = jnp.arange(N_EXPERTS, dtype=i32)[None, :]
    e0 = rinfo[2].astype(i32)
    e1 = rinfo[3].astype(i32)
    dest0 = rinfo[4].astype(i32) + jnp.sum(jnp.where(e0[:, None] == ids, starts[None, :], 0), axis=1)
    dest1 = rinfo[5].astype(i32) + jnp.sum(jnp.where(e1[:, None] == ids, starts[None, :], 0), axis=1)
    nb = n_tok // MOE_TOKEN_TILE
    dest = jnp.concatenate([dest0.reshape(nb, 1, MOE_TOKEN_TILE), dest1.reshape(nb, 1, MOE_TOKEN_TILE)], axis=2)
    tile_start = jnp.arange(n_rows // tm, dtype=i32) * tm
    tile_expert = jnp.minimum(jnp.sum((tile_start[:, None] >= ends[None, :]).astype(i32), axis=1),
                              N_EXPERTS - 1).astype(i32)
    tile_valid = (tile_start < ends[-1]).astype(i32)
    tile_first = (jnp.sum((tile_start[:, None] == starts[None, :]).astype(i32), axis=1) > 0).astype(i32) * tile_valid
    tail = ends[-1] + jnp.arange(N_EXPERTS, dtype=i32) * tm
    zstart = jnp.concatenate([jnp.where(padded > 0, ends - tm, -1),
                              jnp.where(tail < n_rows, tail, -1)]).astype(i32)
    return dest, zstart, tile_expert, tile_valid, tile_first, n_rows


def _dispatch_kernel(zstart_ref, dest_ref, x1r_ref, xs_hbm, zbuf, sem, zsem, *, tt):
    @pl.when(pl.program_id(0) == 0)
    def _():
        zbuf[...] = jnp.zeros_like(zbuf)

        def zero_copy(e):
            z0 = pl.multiple_of(jnp.maximum(zstart_ref[e], 0) * ROW_TILE, ROW_TILE)
            return pltpu.make_async_copy(zbuf, xs_hbm.at[pl.ds(z0, MOE_ROW_TILE * ROW_TILE), :], zsem)

        for e in range(2 * N_EXPERTS):
            @pl.when(zstart_ref[e] >= 0)
            def _():
                zero_copy(e).start()
        for e in range(2 * N_EXPERTS):
            @pl.when(zstart_ref[e] >= 0)
            def _():
                zero_copy(e).wait()

    def row_copy(r, j):
        d = dest_ref[0, 0, j]
        return pltpu.make_async_copy(
            x1r_ref.at[pl.ds(pl.multiple_of(r * ROW_TILE, ROW_TILE), ROW_TILE), :],
            xs_hbm.at[pl.ds(pl.multiple_of(d * ROW_TILE, ROW_TILE), ROW_TILE), :], sem)

    def issue(g, c):
        for u in range(DMA_UNROLL):
            r = g * DMA_UNROLL + u
            row_copy(r, r).start(priority=0)
            row_copy(r, tt + r).start(priority=1)
        return c

    lax.fori_loop(0, tt // DMA_UNROLL, issue, 0)
    whole = pltpu.make_async_copy(x1r_ref, xs_hbm.at[pl.ds(0, tt * ROW_TILE), :], sem)
    whole.wait()
    whole.wait()


def _dispatch(x1r, dest, zstart, n_tok, n_rows):
    tt = MOE_TOKEN_TILE
    est = 2 * tt * D_MODEL * 4 + MOE_ROW_TILE * D_MODEL * 4 + 8 * 1024 * 1024
    grid_spec = pltpu.PrefetchScalarGridSpec(
        num_scalar_prefetch=1,
        grid=(n_tok // tt,),
        in_specs=[pl.BlockSpec((1, 1, 2 * tt), lambda i, z: (i, 0, 0), memory_space=pltpu.SMEM),
                  pl.BlockSpec((tt * ROW_TILE, V7X_LANES), lambda i, z: (i, 0))],
        out_specs=pl.BlockSpec(memory_space=pl.ANY),
        scratch_shapes=[pltpu.VMEM((MOE_ROW_TILE * ROW_TILE, V7X_LANES), f32),
                        pltpu.SemaphoreType.DMA(()),
                        pltpu.SemaphoreType.DMA(())],
    )
    return pl.pallas_call(
        functools.partial(_dispatch_kernel, tt=tt),
        grid_spec=grid_spec,
        out_shape=jax.ShapeDtypeStruct((n_rows * ROW_TILE, V7X_LANES), f32),
        compiler_params=_vmem_params(est, ("arbitrary",)),
        name="dispatch",
    )(zstart, dest, x1r)


def _expert_kernel(texp_ref, tvalid_ref, tfirst_ref, xs_ref, w1_ref, w3_ref, w2_ref, ys_ref,
                   w1b, w3b, w2b, *, tm):
    i = pl.program_id(0)

    @pl.when(tfirst_ref[i] > 0)
    def _():
        w1b[...] = w1_ref[0].astype(bf16)
        w3b[...] = w3_ref[0].astype(bf16)
        w2b[...] = w2_ref[0].astype(bf16)

    @pl.when(tvalid_ref[i] > 0)
    def _():
        xb = _tiles_to_rows_load(xs_ref, tm).astype(bf16)
        a = jnp.dot(xb, w1b[...], preferred_element_type=f32)
        b = jnp.dot(xb, w3b[...], preferred_element_type=f32)
        h = (a * (1.0 / (1.0 + jnp.exp(-a)))) * b
        o = jnp.dot(h.astype(bf16), w2b[...], preferred_element_type=f32)
        _rows_to_tiles_store(ys_ref, o, tm)

    @pl.when(tvalid_ref[i] == 0)
    def _():
        ys_ref[...] = jnp.zeros_like(ys_ref)


def _experts(xs, tile_expert, tile_valid, tile_first, w1, w3, w2, layer, n_rows):
    tm = MOE_ROW_TILE
    base = layer * N_EXPERTS
    blk = (tm * ROW_TILE, V7X_LANES)
    est = 2 * (3 * D_MODEL * D_EXPERT * 4 + 2 * tm * D_MODEL * 4) + 3 * D_MODEL * D_EXPERT * 2 + 16 * 1024 * 1024
    grid_spec = pltpu.PrefetchScalarGridSpec(
        num_scalar_prefetch=3,
        grid=(n_rows // tm,),
        in_specs=[pl.BlockSpec(blk, lambda i, te, tv, tf: (i, 0)),
                  pl.BlockSpec((1, D_MODEL, D_EXPERT), lambda i, te, tv, tf: (base + te[i], 0, 0)),
                  pl.BlockSpec((1, D_MODEL, D_EXPERT), lambda i, te, tv, tf: (base + te[i], 0, 0)),
                  pl.BlockSpec((1, D_EXPERT, D_MODEL), lambda i, te, tv, tf: (base + te[i], 0, 0))],
        out_specs=pl.BlockSpec(blk, lambda i, te, tv, tf: (i, 0)),
        scratch_shapes=[pltpu.VMEM((D_MODEL, D_EXPERT), bf16),
                        pltpu.VMEM((D_MODEL, D_EXPERT), bf16),
                        pltpu.VMEM((D_EXPERT, D_MODEL), bf16)],
    )
    return pl.pallas_call(
        functools.partial(_expert_kernel, tm=tm),
        grid_spec=grid_spec,
        out_shape=jax.ShapeDtypeStruct((n_rows * ROW_TILE, V7X_LANES), f32),
        compiler_params=_vmem_params(est, ("arbitrary",)),
        name="experts",
    )(tile_expert, tile_valid, tile_first, xs, w1, w3, w2)


def _combine_kernel(dest_ref, dest_next_ref, ys_hbm, x1r_ref, rinfo_ref, g_ref, b_ref, o_ref,
                    buf, sem, *, tt, alpha):
    i = pl.program_id(0)
    n = pl.num_programs(0)
    slot = lax.rem(i, 2)
    nslot = 1 - slot

    def row_copy(d_ref, s, j):
        d = d_ref[0, 0, j]
        return pltpu.make_async_copy(
            ys_hbm.at[pl.ds(pl.multiple_of(d * ROW_TILE, ROW_TILE), ROW_TILE), :],
            buf.at[s, pl.ds(pl.multiple_of(j * ROW_TILE, ROW_TILE), ROW_TILE), :], sem.at[s])

    def wait_slot(s):
        pltpu.make_async_copy(ys_hbm.at[pl.ds(0, 2 * tt * ROW_TILE), :], buf.at[s], sem.at[s]).wait()

    @pl.when(i == 0)
    def _():
        def issue(g, c):
            for u in range(DMA_UNROLL):
                row_copy(dest_ref, 0, g * DMA_UNROLL + u).start(priority=u % 2)
            return c
        lax.fori_loop(0, 2 * tt // DMA_UNROLL, issue, 0)

    wait_slot(slot)
    gates = jnp.transpose(rinfo_ref[...])
    tc = tt // COMBINE_CHUNKS
    per_chunk = 2 * tt // COMBINE_CHUNKS
    for c in range(COMBINE_CHUNKS):
        for j in range(c * per_chunk, (c + 1) * per_chunk):
            row_copy(dest_next_ref, nslot, j).start(priority=j % 2)
        r0 = c * tc * ROW_TILE
        o0 = _tiles_to_rows_load(buf.at[slot, pl.ds(r0, tc * ROW_TILE), :], tc)
        o1 = _tiles_to_rows_load(buf.at[slot, pl.ds(tt * ROW_TILE + r0, tc * ROW_TILE), :], tc)
        x1 = _tiles_to_rows_load(x1r_ref.at[pl.ds(r0, tc * ROW_TILE), :], tc)
        g = gates[c * tc:(c + 1) * tc]
        ffn = g[:, 0:1] * o0 + g[:, 1:2] * o1
        o_ref[c * tc:(c + 1) * tc, :] = _layer_norm(alpha * x1 + ffn, g_ref[...], b_ref[...])

    @pl.when(i == n - 1)
    def _():
        wait_slot(nslot)


def _combine(ys, dest, x1r, rinfo, ln_g, ln_b, n_tok, alpha):
    tt = MOE_TOKEN_TILE
    nb = n_tok // tt
    est = 2 * (tt * D_MODEL * 4 * 2 + tt * V7X_LANES * 4) + 2 * 2 * tt * D_MODEL * 4 + 24 * 1024 * 1024
    return pl.pallas_call(
        functools.partial(_combine_kernel, tt=tt, alpha=alpha),
        grid=(nb,),
        in_specs=[pl.BlockSpec((1, 1, 2 * tt), lambda i: (i, 0, 0), memory_space=pltpu.SMEM),
                  pl.BlockSpec((1, 1, 2 * tt), lambda i: (jnp.minimum(i + 1, nb - 1), 0, 0),
                               memory_space=pltpu.SMEM),
                  pl.BlockSpec(memory_space=pl.ANY),
                  pl.BlockSpec((tt * ROW_TILE, V7X_LANES), lambda i: (i, 0)),
                  pl.BlockSpec((INFO_ROWS, tt), lambda i: (0, i)),
                  pl.BlockSpec((1, D_MODEL), lambda i: (0, 0)),
                  pl.BlockSpec((1, D_MODEL), lambda i: (0, 0))],
        out_specs=pl.BlockSpec((tt, D_MODEL), lambda i: (i, 0)),
        out_shape=jax.ShapeDtypeStruct((n_tok, D_MODEL), f32),
        scratch_shapes=[pltpu.VMEM((2, 2 * tt * ROW_TILE, V7X_LANES), f32),
                        pltpu.SemaphoreType.DMA((2,))],
        compiler_params=_vmem_params(est, ("arbitrary",)),
        name="combine_ln",
    )(dest, dest, ys, x1r, rinfo, ln_g, ln_b)


def _prep_w_in(w_in_l):
    o = 3 * CONV_WIDTH
    conv = w_in_l[:, :o]
    sbq = w_in_l[:, o:o + SB_WIDTH] * (SB_HEAD_DIM ** -0.5 * LOG2E)
    sbk = w_in_l[:, o + SB_WIDTH:o + 2 * SB_WIDTH]
    sbv = w_in_l[:, o + 2 * SB_WIDTH:o + 3 * SB_WIDTH]
    o = o + 3 * SB_WIDTH
    gq = w_in_l[:, o:o + GLA_KEY_WIDTH]
    gk = w_in_l[:, o + GLA_KEY_WIDTH:o + 2 * GLA_KEY_WIDTH]
    o = o + 2 * GLA_KEY_WIDTH
    gv = w_in_l[:, o:o + GLA_WIDTH]
    gg = w_in_l[:, o + GLA_WIDTH:o + 2 * GLA_WIDTH]
    lr = w_in_l[:, o + 2 * GLA_WIDTH:]
    zk = jnp.zeros((D_MODEL, GLA_KEY_PAD - GLA_KEY_WIDTH), f32)
    zl = jnp.zeros((D_MODEL, V7X_LANES - GLA_GATE_RANK), f32)
    return jnp.concatenate([sbq, sbk, sbv, conv, gq, zk, gk, zk, gv, gg, lr, zl], axis=1).astype(bf16)


def kernel(x, w_in, conv_w, conv_b, gla_w_gate2, gla_b_gate, gla_norm_w, w_out, ln1_g, ln1_b,
           router_w, router_bias, w1, w3, w2, ln2_g, ln2_b):
    batch, seq, d = x.shape
    depth = w_in.shape[0]
    n_tok = batch * seq
    alpha = (2 * depth) ** 0.25

    w1f = w1.reshape(depth * N_EXPERTS, D_MODEL, D_EXPERT)
    w3f = w3.reshape(depth * N_EXPERTS, D_MODEL, D_EXPERT)
    w2f = w2.reshape(depth * N_EXPERTS, D_EXPERT, D_MODEL)
    rwt = router_w.T.astype(bf16)
    rbt = jnp.broadcast_to(router_bias.reshape(N_EXPERTS, 1), (N_EXPERTS, MIX_TILE))

    x2d = x.reshape(n_tok, d)
    for l in range(depth):
        lw = dict(
            conv_w=jnp.pad(conv_w[l], ((0, V7X_SUBLANES - CONV_K), (0, 0))),
            conv_b=conv_b[l].reshape(1, CONV_WIDTH),
            wg2=jnp.pad(gla_w_gate2[l], ((0, V7X_LANES - GLA_GATE_RANK),
                                         (0, GLA_KEY_PAD - GLA_KEY_WIDTH))).astype(bf16),
            bg=jnp.pad(gla_b_gate[l].reshape(1, GLA_KEY_WIDTH), ((0, 0), (0, GLA_KEY_PAD - GLA_KEY_WIDTH))),
            normw=jnp.tile(gla_norm_w[l], GLA_HEADS).reshape(1, GLA_WIDTH),
            w_out=w_out[l].astype(bf16),
            ln1_g=ln1_g[l].reshape(1, d), ln1_b=ln1_b[l].reshape(1, d),
            rwt=rwt, rbt=rbt)
        proj_sb, proj_rest = _in_proj(x2d, _prep_w_in(w_in[l]))
        ysb = _sb_attention(proj_sb, batch, seq)
        x1r, rinfo, counts = _mix(proj_rest, ysb, x2d, lw, batch, seq, alpha)
        dest, zstart, tile_expert, tile_valid, tile_first, n_rows = _dest_plan(rinfo, counts, n_tok)
        xs = _dispatch(x1r, dest, zstart, n_tok, n_rows)
        ys = _experts(xs, tile_expert, tile_valid, tile_first, w1f, w3f, w2f, l, n_rows)
        x2d = _combine(ys, dest, x1r, rinfo, ln2_g[l].reshape(1, d), ln2_b[l].reshape(1, d), n_tok, alpha)
    return x2d.reshape(batch, seq, d)
```

```python
import functools

import jax
import jax.numpy as jnp
from jax import lax
from jax.experimental import pallas as pl
from jax.experimental.pallas import tpu as pltpu

f32 = jnp.float32
bf16 = jnp.bfloat16
i32 = jnp.int32

V7X_LANES = 128
V7X_SUBLANES = 8
V7X_VMEM_BYTES = 64 * 1024 * 1024

D_MODEL = 1024
CONV_WIDTH = 256
CONV_K = 3
SB_HEADS = 6
SB_HEAD_DIM = 64
SB_WIDTH = SB_HEADS * SB_HEAD_DIM
SB_BLOCK = 128
SB_QT = 512
SB_KT = 2 * SB_BLOCK
SB_SEQS = 2
GLA_HEADS = 4
GLA_DV = 96
GLA_DK = 48
GLA_WIDTH = GLA_HEADS * GLA_DV
GLA_KEY_WIDTH = GLA_HEADS * GLA_DK
GLA_KEY_PAD = 256
GLA_GATE_RANK = 16
GLA_GATE_NORM = 16.0
GLA_CHUNK = 64
N_EXPERTS = 16
N_GROUPS = 4
EXPERTS_PER_GROUP = 4
D_EXPERT = 512
LN_EPS = 1e-5
RMS_EPS = 1e-6
LOG2E = 1.4426950408889634

SB_PROJ_WIDTH = 3 * SB_WIDTH
OFF_CONV = 0
OFF_GQ = 3 * CONV_WIDTH
OFF_GK = OFF_GQ + GLA_KEY_PAD
OFF_GV = OFF_GK + GLA_KEY_PAD
OFF_GG = OFF_GV + GLA_WIDTH
OFF_LR = OFF_GG + GLA_WIDTH
R_WIDTH = OFF_LR + V7X_LANES
P_WIDTH = SB_PROJ_WIDTH + R_WIDTH

ROW_TILE = V7X_SUBLANES
MIX_TILE = 1024
MOE_TOKEN_TILE = 512
MOE_ROW_TILE = 512
DMA_UNROLL = 8
MXU_CHUNK = 256
COMBINE_CHUNKS = 8
INFO_ROWS = V7X_SUBLANES


def _vmem_params(est_bytes, semantics):
    limit = int(min(V7X_VMEM_BYTES - 6 * 1024 * 1024, max(est_bytes, 16 * 1024 * 1024)))
    return pltpu.CompilerParams(dimension_semantics=semantics, vmem_limit_bytes=limit)


def _neg_abs(x):
    bits = lax.bitcast_convert_type(x, jnp.uint32) | jnp.uint32(0x80000000)
    return lax.bitcast_convert_type(bits, f32)


def _rows_to_tiles_store(ref, val, n_rows):
    for j in range(D_MODEL // V7X_LANES):
        ref[pl.ds(j, n_rows, stride=ROW_TILE), :] = val[:, j * V7X_LANES:(j + 1) * V7X_LANES]


def _tiles_to_rows_load(ref, n_rows):
    return jnp.concatenate(
        [ref[pl.ds(j, n_rows, stride=ROW_TILE), :] for j in range(D_MODEL // V7X_LANES)], axis=1)


def _layer_norm(h, g, b):
    mu = jnp.mean(h, axis=-1, keepdims=True)
    c = h - mu
    var = jnp.mean(c * c, axis=-1, keepdims=True)
    return c * lax.rsqrt(var + LN_EPS) * g + b


def _project(xb, w_ref, o_ref, width):
    for n0 in range(0, width, MXU_CHUNK):
        n1 = min(n0 + MXU_CHUNK, width)
        o_ref[:, n0:n1] = jnp.dot(xb, w_ref[:, n0:n1], preferred_element_type=f32).astype(bf16)


def _in_proj_kernel(x_ref, w_ref, osb_ref):
    _project(x_ref[...].astype(bf16), w_ref, osb_ref, SB_PROJ_WIDTH)


def _in_proj(x2d, w):
    t = x2d.shape[0]
    tm = 512
    est = 2 * (tm * D_MODEL * 4 + D_MODEL * SB_PROJ_WIDTH * 2 + tm * SB_PROJ_WIDTH * 2) + 8 * 1024 * 1024
    return pl.pallas_call(
        _in_proj_kernel,
        grid=(t // tm,),
        in_specs=[pl.BlockSpec((tm, D_MODEL), lambda i: (i, 0)),
                  pl.BlockSpec((D_MODEL, SB_PROJ_WIDTH), lambda i: (0, 0))],
        out_specs=pl.BlockSpec((tm, SB_PROJ_WIDTH), lambda i: (i, 0)),
        out_shape=jax.ShapeDtypeStruct((t, SB_PROJ_WIDTH), bf16),
        compiler_params=_vmem_params(est, ("arbitrary",)),
        name="in_proj",
    )(x2d, w)


def _sb_kernel(q_ref, k_ref, v_ref, o_ref, acc_ref, car_ref, *, seq):
    blk = SB_BLOCK
    qt = SB_QT
    n_pairs = SB_WIDTH // V7X_LANES
    klane = lax.broadcasted_iota(i32, (blk, V7X_LANES), 1)
    head_mask = [klane < SB_HEAD_DIM, klane >= SB_HEAD_DIM]
    wr = lax.broadcasted_iota(i32, (2 * blk, 2 * blk), 0)
    wc = lax.broadcasted_iota(i32, (2 * blk, 2 * blk), 1)
    suffix_w = jnp.where(((wr >= blk) == (wc >= blk)) & (wr >= wc), 1.0, 0.0).astype(bf16)
    def softplus2(z):
        return jnp.maximum(z, 0.0) + jnp.log2(1.0 + jnp.exp2(_neg_abs(z)))

    def lane_bcast(tot):
        n = tot.shape[0]
        return jnp.concatenate([jnp.broadcast_to(tot[:, 0:1], (n, blk)),
                                jnp.broadcast_to(tot[:, blk:blk + 1], (n, blk))], axis=1)

    def span(q_tiles, k0, diag_j):
        masked = diag_j is not None
        first = masked and diag_j == qt // SB_KT - 1
        r0 = diag_j * SB_KT if masked else 0
        nr = qt - r0
        if masked:
            ri = lax.broadcasted_iota(i32, (nr, 4 * blk), 0)
            ci = lax.broadcasted_iota(i32, (nr, 4 * blk), 1)
            diag_valid = (ci & (blk - 1)) + jnp.where(ci >= 2 * blk, blk, 0) < ri
        for sq in range(SB_SEQS):
            base = sq * seq
            k_a = k_ref[pl.ds(base + k0, blk), :]
            k_b = k_ref[pl.ds(base + k0 + blk, blk), :]
            v_a = v_ref[pl.ds(base + k0, blk), :]
            v_b = v_ref[pl.ds(base + k0 + blk, blk), :]
            for p in range(n_pairs):
                sl = slice(p * V7X_LANES, (p + 1) * V7X_LANES)
                u = sq * n_pairs + p

                def stack(x_a, x_b):
                    zero = jnp.zeros((blk, V7X_LANES), bf16)
                    return jnp.concatenate([jnp.where(head_mask[0], x_a[:, sl], zero),
                                            jnp.where(head_mask[1], x_a[:, sl], zero),
                                            jnp.where(head_mask[0], x_b[:, sl], zero),
                                            jnp.where(head_mask[1], x_b[:, sl], zero)], axis=0)

                z = lax.dot_general(q_tiles[u][r0:], stack(k_a, k_b), (((1,), (1,)), ((), ())),
                                    preferred_element_type=f32)
                sp = softplus2(z)
                if masked:
                    sp = jnp.where(diag_valid, sp, 0.0)
                spb = sp.astype(bf16)
                tot_b = jnp.dot(spb[:, 2 * blk:], suffix_w, preferred_element_type=f32)
                if not first:
                    tot_b = tot_b + car_ref[u, r0:, :]
                tot_a = jnp.dot(spb[:, :2 * blk], suffix_w, preferred_element_type=f32) + lane_bcast(tot_b)
                car_ref[u, r0:, :] = lane_bcast(tot_a)
                a = jnp.exp2(z - jnp.concatenate([tot_a, tot_b], axis=1))
                if masked:
                    a = jnp.where(diag_valid, a, 0.0)
                av = jnp.dot(a.astype(bf16), stack(v_a, v_b), preferred_element_type=f32)
                if first:
                    acc_ref[u, r0:, :] = av
                    if r0 > 0:
                        acc_ref[u, :r0, :] = jnp.zeros((r0, V7X_LANES), f32)
                        car_ref[u, :r0, :] = jnp.zeros((r0, 2 * blk), f32)
                else:
                    acc_ref[u, r0:, :] += av

    def q_tile(qi, carry):
        q0 = pl.multiple_of(qi * qt, qt)
        q_tiles = []
        for sq in range(SB_SEQS):
            q_all = q_ref[pl.ds(sq * seq + q0, qt), :]
            q_tiles += [q_all[:, p * V7X_LANES:(p + 1) * V7X_LANES] for p in range(n_pairs)]
        for j in reversed(range(qt // SB_KT)):
            span(q_tiles, q0 + j * SB_KT, j)

        def k_span(i, c):
            span(q_tiles, pl.multiple_of(q0 - (i + 1) * SB_KT, SB_KT), None)
            return c

        lax.fori_loop(0, qi * (qt // SB_KT), k_span, 0)
        for sq in range(SB_SEQS):
            o_ref[pl.ds(sq * seq + q0, qt), :] = jnp.concatenate(
                [acc_ref[sq * n_pairs + p] for p in range(n_pairs)], axis=1).astype(o_ref.dtype)
        return carry

    lax.fori_loop(0, seq // qt, q_tile, 0)


def _sb_attention(proj_sb, batch, seq):
    t = batch * seq
    n_pairs = SB_WIDTH // V7X_LANES
    rows = SB_SEQS * seq
    est = 2 * 4 * rows * SB_WIDTH * 2 + SB_SEQS * n_pairs * SB_QT * 3 * V7X_LANES * 4 + 24 * 1024 * 1024
    return pl.pallas_call(
        functools.partial(_sb_kernel, seq=seq),
        grid=(batch // SB_SEQS,),
        in_specs=[pl.BlockSpec((rows, SB_WIDTH), lambda b: (b, 0)),
                  pl.BlockSpec((rows, SB_WIDTH), lambda b: (b, 1)),
                  pl.BlockSpec((rows, SB_WIDTH), lambda b: (b, 2))],
        out_specs=pl.BlockSpec((rows, SB_WIDTH), lambda b: (b, 0)),
        out_shape=jax.ShapeDtypeStruct((t, SB_WIDTH), bf16),
        scratch_shapes=[pltpu.VMEM((SB_SEQS * n_pairs, SB_QT, V7X_LANES), f32),
                        pltpu.VMEM((SB_SEQS * n_pairs, SB_QT, 2 * SB_BLOCK), f32)],
        compiler_params=_vmem_params(est, ("arbitrary",)),
        name="sb_attention",
    )(proj_sb, proj_sb, proj_sb)


def _route(sel, scores):
    c, s = sel, scores
    group_score = []
    for g in range(N_GROUPS):
        a, b, cc, d = c[4 * g:4 * g + 4]
        hi1, lo1 = jnp.maximum(a, b), jnp.minimum(a, b)
        hi2, lo2 = jnp.maximum(cc, d), jnp.minimum(cc, d)
        top1 = jnp.maximum(hi1, hi2)
        top2 = jnp.maximum(jnp.minimum(hi1, hi2), jnp.maximum(lo1, lo2))
        group_score.append(top1 + top2)
    best = group_score[0]
    gidx = jnp.zeros_like(best, dtype=i32)
    for g in range(1, N_GROUPS):
        better = group_score[g] > best
        gidx = jnp.where(better, g, gidx)
        best = jnp.where(better, group_score[g], best)
    vals, scs = [], []
    for j in range(EXPERTS_PER_GROUP):
        vj, sj = c[j], s[j]
        for g in range(1, N_GROUPS):
            pick = gidx == g
            vj = jnp.where(pick, c[4 * g + j], vj)
            sj = jnp.where(pick, s[4 * g + j], sj)
        vals.append(vj)
        scs.append(sj)
    m1, i1, w1 = vals[0], jnp.zeros_like(gidx), scs[0]
    for j in range(1, EXPERTS_PER_GROUP):
        better = vals[j] > m1
        m1 = jnp.where(better, vals[j], m1)
        i1 = jnp.where(better, j, i1)
        w1 = jnp.where(better, scs[j], w1)
    m2 = jnp.full_like(m1, -jnp.inf)
    i2 = jnp.full_like(i1, -1)
    w2 = jnp.zeros_like(w1)
    for j in range(EXPERTS_PER_GROUP):
        better = (i1 != j) & ((vals[j] > m2) | (i2 < 0))
        m2 = jnp.where(better, vals[j], m2)
        i2 = jnp.where(better, j, i2)
        w2 = jnp.where(better, scs[j], w2)
    denom = w1 + w2
    return w1 / denom, w2 / denom, gidx * EXPERTS_PER_GROUP + i1, gidx * EXPERTS_PER_GROUP + i2


def _mix_kernel(wrest_ref, ysb_ref, x_ref, convw_ref, convb_ref, wg2_ref, bg_ref, normw_ref,
                wout_ref, lng_ref, lnb_ref, rwt_ref, rbt_ref,
                x1r_ref, rinfo_ref, counts_ref,
                halo_ref, state_ref, cnt_ref, proj_ref, *, ts, alpha):
    t_idx = pl.program_id(1)
    _project(x_ref[...].astype(bf16), wrest_ref, proj_ref, R_WIDTH)
    kp = GLA_KEY_PAD
    ck = GLA_CHUNK

    @pl.when((pl.program_id(0) == 0) & (t_idx == 0))
    def _():
        cnt_ref[...] = jnp.zeros_like(cnt_ref)

    @pl.when(t_idx == 0)
    def _():
        halo_ref[...] = jnp.zeros_like(halo_ref)
        state_ref[...] = jnp.zeros_like(state_ref)

    c_h = proj_ref[:, OFF_CONV:OFF_CONV + CONV_WIDTH].astype(f32)
    c_b = proj_ref[:, OFF_CONV + CONV_WIDTH:OFF_CONV + 2 * CONV_WIDTH].astype(f32)
    c_c = proj_ref[:, OFF_CONV + 2 * CONV_WIDTH:OFF_CONV + 3 * CONV_WIDTH].astype(f32)
    u = c_c * c_h
    ucat = jnp.concatenate([halo_ref[...], u], axis=0)
    halo_ref[...] = u[ts - V7X_SUBLANES:, :]
    u1 = ucat[V7X_SUBLANES - 1:V7X_SUBLANES - 1 + ts, :]
    u2 = ucat[V7X_SUBLANES - 2:V7X_SUBLANES - 2 + ts, :]
    y = convb_ref[...] + convw_ref[0:1, :] * u
    y = y + convw_ref[1:2, :] * u1
    y = y + convw_ref[2:3, :] * u2
    y_conv = c_b * y

    lr = proj_ref[:, OFF_LR:OFF_LR + V7X_LANES]
    pre = jnp.dot(lr, wg2_ref[...], preferred_element_type=f32) + bg_ref[...]
    log_a = (jnp.minimum(pre, 0.0) - jnp.log1p(jnp.exp(_neg_abs(pre)))) * (1.0 / GLA_GATE_NORM)
    chunk_shift = ck.bit_length() - 1
    dck = 2 * ck
    ir = lax.broadcasted_iota(i32, (dck, dck), 0)
    ic = lax.broadcasted_iota(i32, (dck, dck), 1)
    same_chunk = (ir >> chunk_shift) == (ic >> chunk_shift)
    intra_mask = same_chunk & (ic <= ir)
    cum_w = jnp.where(intra_mask, 1.0, 0.0).astype(bf16)
    tot_w = jnp.where(same_chunk, 1.0, 0.0).astype(bf16)
    cum_tot_w = jnp.concatenate([jnp.concatenate([cum_w, cum_w], axis=1),
                                 jnp.concatenate([tot_w, tot_w], axis=1)], axis=0)
    la_hi = log_a.astype(bf16)
    la_lo = (log_a - la_hi.astype(f32)).astype(bf16)
    bcum, blast = [], []
    for j in range(ts // dck):
        rows = slice(j * dck, (j + 1) * dck)
        r = jnp.dot(cum_tot_w, jnp.concatenate([la_hi[rows], la_lo[rows]], axis=0),
                    preferred_element_type=f32)
        bcum.append(r[:dck])
        blast.append(r[dck:])
    bcum = jnp.concatenate(bcum, axis=0)
    blast = jnp.concatenate(blast, axis=0)
    qf = proj_ref[:, OFF_GQ:OFF_GQ + kp].astype(f32)
    kf = proj_ref[:, OFF_GK:OFF_GK + kp].astype(f32)
    vb = proj_ref[:, OFF_GV:OFF_GV + GLA_WIDTH]
    q_in = ((qf * (GLA_DK ** -0.5)) * jnp.exp(bcum)).astype(bf16)
    k_in = (kf * jnp.exp(-bcum)).astype(bf16)
    k_te = (kf * jnp.exp(blast - bcum)).astype(bf16)
    decay = jnp.exp(blast)

    klane = lax.broadcasted_iota(i32, (dck, kp), 1)
    vlane = lax.broadcasted_iota(i32, (dck, GLA_WIDTH), 1)
    kmask = [(klane >= h * GLA_DK) & (klane < (h + 1) * GLA_DK) for h in range(GLA_HEADS)]
    vmask = [(vlane >= h * GLA_DV) & (vlane < (h + 1) * GLA_DV) for h in range(GLA_HEADS)]
    sr = lax.broadcasted_iota(i32, (GLA_HEADS * dck, dck), 0) & (dck - 1)
    sc_ = lax.broadcasted_iota(i32, (GLA_HEADS * dck, dck), 1)
    stacked_mask = ((sr >> chunk_shift) == (sc_ >> chunk_shift)) & (sc_ <= sr)
    o_intra = []
    for j in range(ts // dck):
        rows = slice(j * dck, (j + 1) * dck)
        q_j, k_j, v_j = q_in[rows], k_in[rows], vb[rows]
        q_st = jnp.concatenate([jnp.where(kmask[h], q_j, jnp.zeros_like(q_j)) for h in range(GLA_HEADS)],
                               axis=0)
        sc = lax.dot_general(q_st, k_j, (((1,), (1,)), ((), ())), preferred_element_type=f32)
        sc = jnp.where(stacked_mask, sc, 0.0).astype(bf16)
        oh = jnp.dot(sc, v_j, preferred_element_type=f32)
        o_j = oh[:dck]
        for h in range(1, GLA_HEADS):
            o_j = jnp.where(vmask[h], oh[h * dck:(h + 1) * dck], o_j)
        o_intra.append(o_j)
    o_intra = jnp.concatenate(o_intra, axis=0)

    srow = lax.broadcasted_iota(i32, (GLA_WIDTH, kp), 0)
    scol = lax.broadcasted_iota(i32, (GLA_WIDTH, kp), 1)
    blockdiag = jnp.zeros((GLA_WIDTH, kp), jnp.bool_)
    for h in range(GLA_HEADS):
        blockdiag = blockdiag | ((srow >= h * GLA_DV) & (srow < (h + 1) * GLA_DV)
                                 & (scol >= h * GLA_DK) & (scol < (h + 1) * GLA_DK))
    kvt = [lax.dot_general(vb[c * ck:(c + 1) * ck], k_te[c * ck:(c + 1) * ck], (((0,), (0,)), ((), ())),
                           preferred_element_type=f32) for c in range(ts // ck)]
    state = state_ref[...]
    o_inter = []
    for c in range(ts // ck):
        o_inter.append(lax.dot_general(q_in[c * ck:(c + 1) * ck], state.astype(bf16),
                                       (((1,), (1,)), ((), ())), preferred_element_type=f32))
        state = decay[c * ck:c * ck + 1, :] * state + jnp.where(blockdiag, kvt[c], 0.0)
    state_ref[...] = state
    o_all = o_intra + jnp.concatenate(o_inter, axis=0)

    o_sq = o_all * o_all
    vlane_t = lax.broadcasted_iota(i32, (ts, GLA_WIDTH), 1)
    inv = jnp.zeros((ts, GLA_WIDTH), f32)
    for h in range(GLA_HEADS):
        hm = (vlane_t >= h * GLA_DV) & (vlane_t < (h + 1) * GLA_DV)
        ms = jnp.sum(jnp.where(hm, o_sq, 0.0), axis=-1, keepdims=True) * (1.0 / GLA_DV)
        inv = jnp.where(hm, lax.rsqrt(ms + RMS_EPS), inv)
    gg = proj_ref[:, OFF_GG:OFF_GG + GLA_WIDTH].astype(f32)
    y_gla = (o_all * inv * normw_ref[...]) * (gg * (1.0 / (1.0 + jnp.exp(-gg))))

    mix = jnp.concatenate([y_conv.astype(bf16), ysb_ref[...], y_gla.astype(bf16)], axis=1)
    proj_out = jnp.dot(mix, wout_ref[...], preferred_element_type=f32)
    x1 = _layer_norm(alpha * x_ref[...] + proj_out, lng_ref[...], lnb_ref[...])
    _rows_to_tiles_store(x1r_ref, x1, ts)

    logits = lax.dot_general(rwt_ref[...], x1.astype(bf16), (((1,), (1,)), ((), ())),
                             preferred_element_type=f32)
    scores = 1.0 / (1.0 + jnp.exp(-logits))
    sel = scores + rbt_ref[...]
    g0, g1, e0, e1 = _route([sel[e:e + 1, :] for e in range(N_EXPERTS)],
                            [scores[e:e + 1, :] for e in range(N_EXPERTS)])
    eid = lax.broadcasted_iota(i32, (N_EXPERTS, ts), 0)
    pick0 = eid == e0
    pick1 = eid == e1
    onehot = jnp.where(pick0 | pick1, 1.0, 0.0)
    rr = lax.broadcasted_iota(i32, (ts, ts), 0)
    cc = lax.broadcasted_iota(i32, (ts, ts), 1)
    before = jnp.dot(onehot.astype(bf16), jnp.where(rr < cc, 1.0, 0.0).astype(bf16),
                     preferred_element_type=f32) + cnt_ref[...]
    rank0 = jnp.sum(jnp.where(pick0, before, 0.0), axis=0, keepdims=True)
    rank1 = jnp.sum(jnp.where(pick1, before, 0.0), axis=0, keepdims=True)
    new_cnt = cnt_ref[...] + jnp.sum(onehot, axis=1, keepdims=True)
    cnt_ref[...] = new_cnt
    counts_ref[...] = new_cnt[:, :V7X_LANES]
    zero = jnp.zeros_like(g0)
    rinfo_ref[...] = jnp.concatenate(
        [g0, g1, e0.astype(f32), e1.astype(f32), rank0, rank1, zero, zero], axis=0)


def _mix(w_rest, ysb, x2d, lw, batch, seq, alpha):
    t = batch * seq
    ts = MIX_TILE
    nt = seq // ts
    row = lambda b, s: (b * nt + s, 0)
    const = lambda b, s: (0, 0)
    est = (2 * (D_MODEL * R_WIDTH * 2 + ts * SB_WIDTH * 2 + 2 * ts * D_MODEL * 4 + D_MODEL * D_MODEL * 2)
           + ts * R_WIDTH * 2 + 28 * 1024 * 1024)
    return pl.pallas_call(
        functools.partial(_mix_kernel, ts=ts, alpha=alpha),
        grid=(batch, nt),
        in_specs=[pl.BlockSpec((D_MODEL, R_WIDTH), const),
                  pl.BlockSpec((ts, SB_WIDTH), row),
                  pl.BlockSpec((ts, D_MODEL), row),
                  pl.BlockSpec((V7X_SUBLANES, CONV_WIDTH), const),
                  pl.BlockSpec((1, CONV_WIDTH), const),
                  pl.BlockSpec((V7X_LANES, GLA_KEY_PAD), const),
                  pl.BlockSpec((1, GLA_KEY_PAD), const),
                  pl.BlockSpec((1, GLA_WIDTH), const),
                  pl.BlockSpec((D_MODEL, D_MODEL), const),
                  pl.BlockSpec((1, D_MODEL), const),
                  pl.BlockSpec((1, D_MODEL), const),
                  pl.BlockSpec((N_EXPERTS, D_MODEL), const),
                  pl.BlockSpec((N_EXPERTS, ts), const)],
        out_specs=[pl.BlockSpec((ts * ROW_TILE, V7X_LANES), row),
                   pl.BlockSpec((INFO_ROWS, ts), lambda b, s: (0, b * nt + s)),
                   pl.BlockSpec((N_EXPERTS, V7X_LANES), const)],
        out_shape=[jax.ShapeDtypeStruct((t * ROW_TILE, V7X_LANES), f32),
                   jax.ShapeDtypeStruct((INFO_ROWS, t), f32),
                   jax.ShapeDtypeStruct((N_EXPERTS, V7X_LANES), f32)],
        scratch_shapes=[pltpu.VMEM((V7X_SUBLANES, CONV_WIDTH), f32),
                        pltpu.VMEM((GLA_WIDTH, GLA_KEY_PAD), f32),
                        pltpu.VMEM((N_EXPERTS, ts), f32),
                        pltpu.VMEM((ts, R_WIDTH), bf16)],
        compiler_params=_vmem_params(est, ("arbitrary", "arbitrary")),
        name="mix_ln_router",
    )(w_rest, ysb, x2d, lw["conv_w"], lw["conv_b"], lw["wg2"], lw["bg"], lw["normw"], lw["w_out"],
      lw["ln1_g"], lw["ln1_b"], lw["rwt"], lw["rbt"])


def _dest_plan(rinfo, counts, n_tok):
    tm = MOE_ROW_TILE
    n_rows = 2 * n_tok + N_EXPERTS * tm
    cnt = counts[:, 0].astype(i32)
    padded = ((cnt + tm - 1) // tm) * tm
    ends = jnp.cumsum(padded)
    starts = ends - padded
    ids = jnp.arange(N_EXPERTS, dtype=i32)[None, :]
    e0 = rinfo[2].astype(i32)
    e1 = rinfo[3].astype(i32)
    dest0 = rinfo[4].astype(i32) + jnp.sum(jnp.where(e0[:, None] == ids, starts[None, :], 0), axis=1)
    dest1 = rinfo[5].astype(i32) + jnp.sum(jnp.where(e1[:, None] == ids, starts[None, :], 0), axis=1)
    nb = n_tok // MOE_TOKEN_TILE
    dest = jnp.concatenate([dest0.reshape(nb, 1, MOE_TOKEN_TILE), dest1.reshape(nb, 1, MOE_TOKEN_TILE)], axis=2)
    tile_start = jnp.arange(n_rows // tm, dtype=i32) * tm
    tile_expert = jnp.minimum(jnp.sum((tile_start[:, None] >= ends[None, :]).astype(i32), axis=1),
                              N_EXPERTS - 1).astype(i32)
    tile_valid = (tile_start < ends[-1]).astype(i32)
    tile_first = (jnp.sum((tile_start[:, None] == starts[None, :]).astype(i32), axis=1) > 0).astype(i32) * tile_valid
    tail = ends[-1] + jnp.arange(N_EXPERTS, dtype=i32) * tm
    zstart = jnp.concatenate([jnp.where(padded > 0, ends - tm, -1),
                              jnp.where(tail < n_rows, tail, -1)]).astype(i32)
    return dest, zstart, tile_expert, tile_valid, tile_first, n_rows


def _dispatch_kernel(zstart_ref, dest_ref, x1r_ref, xs_hbm, zbuf, sem, zsem, *, tt):
    @pl.when(pl.program_id(0) == 0)
    def _():
        zbuf[...] = jnp.zeros_like(zbuf)

        def zero_copy(e):
            z0 = pl.multiple_of(jnp.maximum(zstart_ref[e], 0) * ROW_TILE, ROW_TILE)
            return pltpu.make_async_copy(zbuf, xs_hbm.at[pl.ds(z0, MOE_ROW_TILE * ROW_TILE), :], zsem)

        for e in range(2 * N_EXPERTS):
            @pl.when(zstart_ref[e] >= 0)
            def _():
                zero_copy(e).start()
        for e in range(2 * N_EXPERTS):
            @pl.when(zstart_ref[e] >= 0)
            def _():
                zero_copy(e).wait()

    def row_copy(r, j):
        d = dest_ref[0, 0, j]
        return pltpu.make_async_copy(
            x1r_ref.at[pl.ds(pl.multiple_of(r * ROW_TILE, ROW_TILE), ROW_TILE), :],
            xs_hbm.at[pl.ds(pl.multiple_of(d * ROW_TILE, ROW_TILE), ROW_TILE), :], sem)

    def issue(g, c):
        for u in range(DMA_UNROLL):
            r = g * DMA_UNROLL + u
            row_copy(r, r).start(priority=0)
            row_copy(r, tt + r).start(priority=1)
        return c

    lax.fori_loop(0, tt // DMA_UNROLL, issue, 0)
    whole = pltpu.make_async_copy(x1r_ref, xs_hbm.at[pl.ds(0, tt * ROW_TILE), :], sem)
    whole.wait()
    whole.wait()


def _dispatch(x1r, dest, zstart, n_tok, n_rows):
    tt = MOE_TOKEN_TILE
    est = 2 * tt * D_MODEL * 4 + MOE_ROW_TILE * D_MODEL * 4 + 8 * 1024 * 1024
    grid_spec = pltpu.PrefetchScalarGridSpec(
        num_scalar_prefetch=1,
        grid=(n_tok // tt,),
        in_specs=[pl.BlockSpec((1, 1, 2 * tt), lambda i, z: (i, 0, 0), memory_space=pltpu.SMEM),
                  pl.BlockSpec((tt * ROW_TILE, V7X_LANES), lambda i, z: (i, 0))],
        out_specs=pl.BlockSpec(memory_space=pl.ANY),
        scratch_shapes=[pltpu.VMEM((MOE_ROW_TILE * ROW_TILE, V7X_LANES), f32),
                        pltpu.SemaphoreType.DMA(()),
                        pltpu.SemaphoreType.DMA(())],
    )
    return pl.pallas_call(
        functools.partial(_dispatch_kernel, tt=tt),
        grid_spec=grid_spec,
        out_shape=jax.ShapeDtypeStruct((n_rows * ROW_TILE, V7X_LANES), f32),
        compiler_params=_vmem_params(est, ("arbitrary",)),
        name="dispatch",
    )(zstart, dest, x1r)


def _expert_kernel(texp_ref, tvalid_ref, tfirst_ref, xs_ref, w1_ref, w3_ref, w2_ref, ys_ref,
                   w1b, w3b, w2b, *, tm):
    i = pl.program_id(0)

    @pl.when(tfirst_ref[i] > 0)
    def _():
        w1b[...] = w1_ref[0].astype(bf16)
        w3b[...] = w3_ref[0].astype(bf16)
        w2b[...] = w2_ref[0].astype(bf16)

    @pl.when(tvalid_ref[i] > 0)
    def _():
        xb = _tiles_to_rows_load(xs_ref, tm).astype(bf16)
        a = jnp.dot(xb, w1b[...], preferred_element_type=f32)
        b = jnp.dot(xb, w3b[...], preferred_element_type=f32)
        h = (a * (1.0 / (1.0 + jnp.exp(-a)))) * b
        o = jnp.dot(h.astype(bf16), w2b[...], preferred_element_type=f32)
        _rows_to_tiles_store(ys_ref, o, tm)

    @pl.when(tvalid_ref[i] == 0)
    def _():
        ys_ref[...] = jnp.zeros_like(ys_ref)


def _experts(xs, tile_expert, tile_valid, tile_first, w1, w3, w2, layer, n_rows):
    tm = MOE_ROW_TILE
    base = layer * N_EXPERTS
    blk = (tm * ROW_TILE, V7X_LANES)
    est = 2 * (3 * D_MODEL * D_EXPERT * 4 + 2 * tm * D_MODEL * 4) + 3 * D_MODEL * D_EXPERT * 2 + 16 * 1024 * 1024
    grid_spec = pltpu.PrefetchScalarGridSpec(
        num_scalar_prefetch=3,
        grid=(n_rows // tm,),
        in_specs=[pl.BlockSpec(blk, lambda i, te, tv, tf: (i, 0)),
                  pl.BlockSpec((1, D_MODEL, D_EXPERT), lambda i, te, tv, tf: (base + te[i], 0, 0)),
                  pl.BlockSpec((1, D_MODEL, D_EXPERT), lambda i, te, tv, tf: (base + te[i], 0, 0)),
                  pl.BlockSpec((1, D_EXPERT, D_MODEL), lambda i, te, tv, tf: (base + te[i], 0, 0))],
        out_specs=pl.BlockSpec(blk, lambda i, te, tv, tf: (i, 0)),
        scratch_shapes=[pltpu.VMEM((D_MODEL, D_EXPERT), bf16),
                        pltpu.VMEM((D_MODEL, D_EXPERT), bf16),
                        pltpu.VMEM((D_EXPERT, D_MODEL), bf16)],
    )
    return pl.pallas_call(
        functools.partial(_expert_kernel, tm=tm),
        grid_spec=grid_spec,
        out_shape=jax.ShapeDtypeStruct((n_rows * ROW_TILE, V7X_LANES), f32),
        compiler_params=_vmem_params(est, ("arbitrary",)),
        name="experts",
    )(tile_expert, tile_valid, tile_first, xs, w1, w3, w2)


def _combine_kernel(dest_ref, dest_next_ref, ys_hbm, x1r_ref, rinfo_ref, g_ref, b_ref, o_ref,
                    buf, sem, *, tt, alpha):
    i = pl.program_id(0)
    n = pl.num_programs(0)
    slot = lax.rem(i, 2)
    nslot = 1 - slot

    def row_copy(d_ref, s, j):
        d = d_ref[0, 0, j]
        return pltpu.make_async_copy(
            ys_hbm.at[pl.ds(pl.multiple_of(d * ROW_TILE, ROW_TILE), ROW_TILE), :],
            buf.at[s, pl.ds(pl.multiple_of(j * ROW_TILE, ROW_TILE), ROW_TILE), :], sem.at[s])

    def wait_slot(s):
        pltpu.make_async_copy(ys_hbm.at[pl.ds(0, 2 * tt * ROW_TILE), :], buf.at[s], sem.at[s]).wait()

    @pl.when(i == 0)
    def _():
        def issue(g, c):
            for u in range(DMA_UNROLL):
                row_copy(dest_ref, 0, g * DMA_UNROLL + u).start(priority=u % 2)
            return c
        lax.fori_loop(0, 2 * tt // DMA_UNROLL, issue, 0)

    wait_slot(slot)
    gates = jnp.transpose(rinfo_ref[...])
    tc = tt // COMBINE_CHUNKS
    per_chunk = 2 * tt // COMBINE_CHUNKS
    for c in range(COMBINE_CHUNKS):
        for j in range(c * per_chunk, (c + 1) * per_chunk):
            row_copy(dest_next_ref, nslot, j).start(priority=j % 2)
        r0 = c * tc * ROW_TILE
        o0 = _tiles_to_rows_load(buf.at[slot, pl.ds(r0, tc * ROW_TILE), :], tc)
        o1 = _tiles_to_rows_load(buf.at[slot, pl.ds(tt * ROW_TILE + r0, tc * ROW_TILE), :], tc)
        x1 = _tiles_to_rows_load(x1r_ref.at[pl.ds(r0, tc * ROW_TILE), :], tc)
        g = gates[c * tc:(c + 1) * tc]
        ffn = g[:, 0:1] * o0 + g[:, 1:2] * o1
        o_ref[c * tc:(c + 1) * tc, :] = _layer_norm(alpha * x1 + ffn, g_ref[...], b_ref[...])

    @pl.when(i == n - 1)
    def _():
        wait_slot(nslot)


def _combine(ys, dest, x1r, rinfo, ln_g, ln_b, n_tok, alpha):
    tt = MOE_TOKEN_TILE
    nb = n_tok // tt
    est = 2 * (tt * D_MODEL * 4 * 2 + tt * V7X_LANES * 4) + 2 * 2 * tt * D_MODEL * 4 + 24 * 1024 * 1024
    return pl.pallas_call(
        functools.partial(_combine_kernel, tt=tt, alpha=alpha),
        grid=(nb,),
        in_specs=[pl.BlockSpec((1, 1, 2 * tt), lambda i: (i, 0, 0), memory_space=pltpu.SMEM),
                  pl.BlockSpec((1, 1, 2 * tt), lambda i: (jnp.minimum(i + 1, nb - 1), 0, 0),
                               memory_space=pltpu.SMEM),
                  pl.BlockSpec(memory_space=pl.ANY),
                  pl.BlockSpec((tt * ROW_TILE, V7X_LANES), lambda i: (i, 0)),
                  pl.BlockSpec((INFO_ROWS, tt), lambda i: (0, i)),
                  pl.BlockSpec((1, D_MODEL), lambda i: (0, 0)),
                  pl.BlockSpec((1, D_MODEL), lambda i: (0, 0))],
        out_specs=pl.BlockSpec((tt, D_MODEL), lambda i: (i, 0)),
        out_shape=jax.ShapeDtypeStruct((n_tok, D_MODEL), f32),
        scratch_shapes=[pltpu.VMEM((2, 2 * tt * ROW_TILE, V7X_LANES), f32),
                        pltpu.SemaphoreType.DMA((2,))],
        compiler_params=_vmem_params(est, ("arbitrary",)),
        name="combine_ln",
    )(dest, dest, ys, x1r, rinfo, ln_g, ln_b)


def _prep_w_in(w_in_l):
    o = 3 * CONV_WIDTH
    conv = w_in_l[:, :o]
    sbq = w_in_l[:, o:o + SB_WIDTH] * (SB_HEAD_DIM ** -0.5 * LOG2E)
    sbk = w_in_l[:, o + SB_WIDTH:o + 2 * SB_WIDTH]
    sbv = w_in_l[:, o + 2 * SB_WIDTH:o + 3 * SB_WIDTH]
    o = o + 3 * SB_WIDTH
    gq = w_in_l[:, o:o + GLA_KEY_WIDTH]
    gk = w_in_l[:, o + GLA_KEY_WIDTH:o + 2 * GLA_KEY_WIDTH]
    o = o + 2 * GLA_KEY_WIDTH
    gv = w_in_l[:, o:o + GLA_WIDTH]
    gg = w_in_l[:, o + GLA_WIDTH:o + 2 * GLA_WIDTH]
    lr = w_in_l[:, o + 2 * GLA_WIDTH:]
    zk = jnp.zeros((D_MODEL, GLA_KEY_PAD - GLA_KEY_WIDTH), f32)
    zl = jnp.zeros((D_MODEL, V7X_LANES - GLA_GATE_RANK), f32)
    w_sb = jnp.concatenate([sbq, sbk, sbv], axis=1).astype(bf16)
    w_rest = jnp.concatenate([conv, gq, zk, gk, zk, gv, gg, lr, zl], axis=1).astype(bf16)
    return w_sb, w_rest


def kernel(x, w_in, conv_w, conv_b, gla_w_gate2, gla_b_gate, gla_norm_w, w_out, ln1_g, ln1_b,
           router_w, router_bias, w1, w3, w2, ln2_g, ln2_b):
    batch, seq, d = x.shape
    depth = w_in.shape[0]
    n_tok = batch * seq
    alpha = (2 * depth) ** 0.25

    w1f = w1.reshape(depth * N_EXPERTS, D_MODEL, D_EXPERT)
    w3f = w3.reshape(depth * N_EXPERTS, D_MODEL, D_EXPERT)
    w2f = w2.reshape(depth * N_EXPERTS, D_EXPERT, D_MODEL)
    rwt = router_w.T.astype(bf16)
    rbt = jnp.broadcast_to(router_bias.reshape(N_EXPERTS, 1), (N_EXPERTS, MIX_TILE))

    x2d = x.reshape(n_tok, d)
    for l in range(depth):
        lw = dict(
            conv_w=jnp.pad(conv_w[l], ((0, V7X_SUBLANES - CONV_K), (0, 0))),
            conv_b=conv_b[l].reshape(1, CONV_WIDTH),
            wg2=jnp.pad(gla_w_gate2[l], ((0, V7X_LANES - GLA_GATE_RANK),
                                         (0, GLA_KEY_PAD - GLA_KEY_WIDTH))).astype(bf16),
            bg=jnp.pad(gla_b_gate[l].reshape(1, GLA_KEY_WIDTH), ((0, 0), (0, GLA_KEY_PAD - GLA_KEY_WIDTH))),
            normw=jnp.tile(gla_norm_w[l], GLA_HEADS).reshape(1, GLA_WIDTH),
            w_out=w_out[l].astype(bf16),
            ln1_g=ln1_g[l].reshape(1, d), ln1_b=ln1_b[l].reshape(1, d),
            rwt=rwt, rbt=rbt)
        w_sb, w_rest = _prep_w_in(w_in[l])
        proj_sb = _in_proj(x2d, w_sb)
        ysb = _sb_attention(proj_sb, batch, seq)
        x1r, rinfo, counts = _mix(w_rest, ysb, x2d, lw, batch, seq, alpha)
        dest, zstart, tile_expert, tile_valid, tile_first, n_rows = _dest_plan(rinfo, counts, n_tok)
        xs = _dispatch(x1r, dest, zstart, n_tok, n_rows)
        ys = _experts(xs, tile_expert, tile_valid, tile_first, w1f, w3f, w2f, l, n_rows)
        x2d = _combine(ys, dest, x1r, rinfo, ln2_g[l].reshape(1, d), ln2_b[l].reshape(1, d), n_tok, alpha)
    return x2d.reshape(batch, seq, d)
```
